```python
import jax, jax.numpy as jnp
from jax import lax
import numpy as np

D_MODEL = 1024
BATCH = 4
SEQ = 8192
DEPTH = 2

HEAD_DIM = 64
N_MIXERS = 2
MOBA_HEADS = 16
MOBA_BLOCK = 256
MOBA_TOPK = 3
MOBA_Q_CHUNK = 32
DIL_GROUPS = ((128, 1), (512, 4), (2048, 16))
DIL_HEADS_PER_GROUP = 4
DIL_HEADS = DIL_HEADS_PER_GROUP * len(DIL_GROUPS)
DIL_BLOCK = 128
D_FF = 4 * D_MODEL
ROPE_THETA = 10000.0
LN_EPS = 1e-5
DEEPNORM_ALPHA = (2.0 * DEPTH) ** 0.25
DEEPNORM_BETA = (8.0 * DEPTH) ** -0.25
N_LAYERS_A = (DEPTH + N_MIXERS - 1) // N_MIXERS
N_LAYERS_B = DEPTH // N_MIXERS

kernel_name = "hybrid_moba_dilated_sqrelu_deepnorm"


def layer_norm(x, g, b):
    xf = x.astype(jnp.float32)
    mu = jnp.mean(xf, axis=-1, keepdims=True)
    var = jnp.mean(jnp.square(xf - mu), axis=-1, keepdims=True)
    y = (xf - mu) * lax.rsqrt(var + LN_EPS)
    return (y * g.astype(jnp.float32) + b.astype(jnp.float32)).astype(x.dtype)


def rotary_tables(seq, dtype):
    inv = 1.0 / (ROPE_THETA ** (jnp.arange(0, HEAD_DIM, 2, dtype=jnp.float32) / HEAD_DIM))
    ang = jnp.arange(seq, dtype=jnp.float32)[:, None] * inv[None, :]
    return jnp.cos(ang).astype(dtype), jnp.sin(ang).astype(dtype)


def apply_rotary(t, cos, sin):
    t1, t2 = jnp.split(t, 2, axis=-1)
    c = cos[None, :, None, :]
    s = sin[None, :, None, :]
    return jnp.concatenate([t1 * c - t2 * s, t2 * c + t1 * s], axis=-1)


def moba_attention(x, w_qkv, w_o, cos, sin):
    B, S, _ = x.shape
    H, dh, BLK, QC = MOBA_HEADS, HEAD_DIM, MOBA_BLOCK, MOBA_Q_CHUNK
    qkv = (x @ w_qkv).reshape(B, S, 3, H, dh)
    q = apply_rotary(qkv[:, :, 0], cos, sin) * (dh ** -0.5)
    k = apply_rotary(qkv[:, :, 1], cos, sin)
    v = qkv[:, :, 2]
    pad = (-S) % BLK
    Sp = S + pad
    nb = Sp // BLK
    padw = ((0, 0), (0, pad), (0, 0), (0, 0))
    q = jnp.pad(q, padw).transpose(0, 2, 1, 3)
    k = jnp.pad(k, padw).transpose(0, 2, 1, 3)
    v = jnp.pad(v, padw).transpose(0, 2, 1, 3)
    k_blk = k.reshape(B, H, nb, BLK, dh)
    v_blk = v.reshape(B, H, nb, BLK, dh)
    k_mean = jnp.mean(k_blk.astype(jnp.float32), axis=3).astype(k.dtype)

    gate = jnp.einsum('bhsd,bhnd->bhsn', q, k_mean).astype(jnp.float32)
    q_block = jnp.arange(Sp) // BLK
    past = jnp.arange(nb)[None, :] < q_block[:, None]
    gate = jnp.where(past[None, None], gate, -jnp.inf)
    kk = min(MOBA_TOPK, nb)
    sel_score, sel_idx = lax.top_k(gate, kk)
    sel_valid = jnp.isfinite(sel_score)

    nc = Sp // QC
    q_c = q.reshape(B, H, nc, QC, dh).transpose(2, 0, 1, 3, 4)
    idx_c = sel_idx.reshape(B, H, nc, QC, kk).transpose(2, 0, 1, 3, 4)
    val_c = sel_valid.reshape(B, H, nc, QC, kk).transpose(2, 0, 1, 3, 4)
    b_ix = jnp.arange(B)[:, None, None, None]
    h_ix = jnp.arange(H)[None, :, None, None]
    key_off = jnp.arange(BLK)

    def one_chunk(args):
        c, qc, idx, valid = args
        kg = k_blk[b_ix, h_ix, idx]
        vg = v_blk[b_ix, h_ix, idx]
        s_past = jnp.einsum('bhqd,bhqnkd->bhqnk', qc, kg).astype(jnp.float32)
        s_past = jnp.where(valid[..., None], s_past, -jnp.inf).reshape(B, H, QC, kk * BLK)
        start = c * QC
        blk = start // BLK
        k_own = lax.dynamic_index_in_dim(k_blk, blk, axis=2, keepdims=False)
        v_own = lax.dynamic_index_in_dim(v_blk, blk, axis=2, keepdims=False)
        s_own = jnp.einsum('bhqd,bhkd->bhqk', qc, k_own).astype(jnp.float32)
        q_off = start % BLK + jnp.arange(QC)
        causal = key_off[None, :] <= q_off[:, None]
        s_own = jnp.where(causal[None, None], s_own, -jnp.inf)
        p = jax.nn.softmax(jnp.concatenate([s_past, s_own], axis=-1), axis=-1).astype(v.dtype)
        p_past = p[..., :kk * BLK].reshape(B, H, QC, kk, BLK)
        p_own = p[..., kk * BLK:]
        return (jnp.einsum('bhqnk,bhqnkd->bhqd', p_past, vg)
                + jnp.einsum('bhqk,bhkd->bhqd', p_own, v_own))

    out = lax.map(one_chunk, (jnp.arange(nc, dtype=jnp.int32), q_c, idx_c, val_c))
    out = out.transpose(1, 0, 3, 2, 4).reshape(B, Sp, H * dh)[:, :S]
    return out @ w_o


def dilated_group(q, k, v, window, dilation):
    B, S, Hg, dh = q.shape
    span = window // dilation
    WB = DIL_BLOCK
    assert span <= WB
    L = S // dilation
    Lp = -(-L // WB) * WB
    nblk = Lp // WB

    def to_blocks(t):
        t = t.reshape(B, L, dilation, Hg, dh).transpose(0, 2, 1, 3, 4)
        t = jnp.pad(t, ((0, 0), (0, 0), (0, Lp - L), (0, 0), (0, 0)))
        return t.reshape(B, dilation, nblk, WB, Hg, dh)

    def with_prev(t):
        prev = jnp.pad(t, ((0, 0), (0, 0), (1, 0), (0, 0), (0, 0), (0, 0)))[:, :, :-1]
        return jnp.concatenate([prev, t], axis=3)

    qb = to_blocks(q)
    kw = with_prev(to_blocks(k))
    vw = with_prev(to_blocks(v))
    s = jnp.einsum('brnqhd,brnkhd->brnhqk', qb, kw).astype(jnp.float32)
    qi = jnp.arange(nblk)[:, None, None] * WB + jnp.arange(WB)[None, :, None]
    ki = jnp.arange(nblk)[:, None, None] * WB - WB + jnp.arange(2 * WB)[None, None, :]
    dist = qi - ki
    mask = (dist >= 0) & (dist <= span) & (ki >= 0)
    s = jnp.where(mask[None, None, :, None], s, -jnp.inf)
    lse = jax.nn.logsumexp(s, axis=-1)
    p = jnp.exp(s - lse[..., None]).astype(v.dtype)
    o = jnp.einsum('brnhqk,brnkhd->brnqhd', p, vw)
    o = o.reshape(B, dilation, Lp, Hg, dh)[:, :, :L].transpose(0, 2, 1, 3, 4).reshape(B, S, Hg, dh)
    lse = lse.transpose(0, 1, 2, 4, 3).reshape(B, dilation, Lp, Hg)[:, :, :L]
    lse = lse.transpose(0, 2, 1, 3).reshape(B, S, Hg)
    return o, lse


def dilated_attention(x, w_qkv, w_o, cos, sin):
    B, S, _ = x.shape
    G, Hg, dh = len(DIL_GROUPS), DIL_HEADS_PER_GROUP, HEAD_DIM
    qkv = (x @ w_qkv).reshape(B, S, 3, DIL_HEADS, dh)
    q = apply_rotary(qkv[:, :, 0], cos, sin) * (dh ** -0.5)
    k = apply_rotary(qkv[:, :, 1], cos, sin)
    v = qkv[:, :, 2]
    outs, lses = [], []
    for g, (window, dilation) in enumerate(DIL_GROUPS):
        hs = slice(g * Hg, (g + 1) * Hg)
        o_g, lse_g = dilated_group(q[:, :, hs], k[:, :, hs], v[:, :, hs], window, dilation)
        outs.append(o_g)
        lses.append(lse_g)
    alpha = jax.nn.softmax(jnp.stack(lses, axis=0), axis=0).astype(x.dtype)
    o = jnp.stack(outs, axis=0) * alpha[..., None]
    o = o.transpose(1, 2, 0, 3, 4).reshape(B, S, G * Hg * dh)
    return o @ w_o


def sq_relu_mlp(x, w_in, w_out):
    return jnp.square(jax.nn.relu(x @ w_in)) @ w_out


def setup_inputs(seed: int = 0) -> dict:
    key = jax.random.key(seed)
    ks = jax.random.split(key, 12)
    f32 = jnp.float32
    moba_w = MOBA_HEADS * HEAD_DIM
    dil_w = DIL_HEADS * HEAD_DIM
    x = jax.random.normal(ks[0], (BATCH, SEQ, D_MODEL), f32)
    moba_w_qkv = jax.random.normal(ks[1], (N_LAYERS_A, D_MODEL, 3 * moba_w), f32) * D_MODEL ** -0.5
    moba_w_o = jax.random.normal(ks[2], (N_LAYERS_A, moba_w, D_MODEL), f32) * (moba_w ** -0.5 * DEEPNORM_BETA)
    dil_w_qkv = jax.random.normal(ks[3], (N_LAYERS_B, D_MODEL, 3 * dil_w), f32) * D_MODEL ** -0.5
    dil_w_o = jax.random.normal(ks[4], (N_LAYERS_B, dil_w, D_MODEL), f32) * (dil_w ** -0.5 * DEEPNORM_BETA)
    mlp_w_in = jax.random.normal(ks[5], (DEPTH, D_MODEL, D_FF), f32) * D_MODEL ** -0.5
    mlp_w_out = jax.random.normal(ks[6], (DEPTH, D_FF, D_MODEL), f32) * (D_FF ** -0.5 * DEEPNORM_BETA)
    ln_mix_g = 1.0 + 0.02 * jax.random.normal(ks[7], (DEPTH, D_MODEL), f32)
    ln_mix_b = 0.02 * jax.random.normal(ks[8], (DEPTH, D_MODEL), f32)
    ln_mlp_g = 1.0 + 0.02 * jax.random.normal(ks[9], (DEPTH, D_MODEL), f32)
    ln_mlp_b = 0.02 * jax.random.normal(ks[10], (DEPTH, D_MODEL), f32)
    return {"x": x, "moba_w_qkv": moba_w_qkv, "moba_w_o": moba_w_o,
            "dil_w_qkv": dil_w_qkv, "dil_w_o": dil_w_o,
            "mlp_w_in": mlp_w_in, "mlp_w_out": mlp_w_out,
            "ln_mix_g": ln_mix_g, "ln_mix_b": ln_mix_b,
            "ln_mlp_g": ln_mlp_g, "ln_mlp_b": ln_mlp_b}


def reference(x, moba_w_qkv, moba_w_o, dil_w_qkv, dil_w_o, mlp_w_in, mlp_w_out,
              ln_mix_g, ln_mix_b, ln_mlp_g, ln_mlp_b):
    S = x.shape[1]
    cos, sin = rotary_tables(S, x.dtype)
    h = x
    for i in range(DEPTH):
        j = i // N_MIXERS
        if i % N_MIXERS == 0:
            mix = moba_attention(h, moba_w_qkv[j], moba_w_o[j], cos, sin)
        else:
            mix = dilated_attention(h, dil_w_qkv[j], dil_w_o[j], cos, sin)
        h = layer_norm(DEEPNORM_ALPHA * h + mix, ln_mix_g[i], ln_mix_b[i])
        h = layer_norm(DEEPNORM_ALPHA * h + sq_relu_mlp(h, mlp_w_in[i], mlp_w_out[i]),
                       ln_mlp_g[i], ln_mlp_b[i])
    return h
```

```python
import functools
import math

import jax
import jax.numpy as jnp
from jax import lax
from jax.experimental import pallas as pl
from jax.experimental.pallas import tpu as pltpu

HEAD_DIM = 64
HALF_DIM = HEAD_DIM // 2
ROPE_THETA = 10000.0
LN_EPS = 1e-5
DEPTH = 2
DEEPNORM_ALPHA = (2.0 * DEPTH) ** 0.25
MOBA_BLOCK = 256
MOBA_TOPK = 3
DIL_GROUPS = ((128, 1), (512, 4), (2048, 16))
DIL_HEADS_PER_GROUP = 4
DIL_BLOCK = 128

LANES = 128
BF16_SUBLANES = 16
VMEM_LIMIT_BYTES = 56 * 1024 * 1024

LOG2E = math.log2(math.e)

F32 = jnp.float32
BF16 = jnp.bfloat16
NEG_INF = float("-inf")
POS_INF = float("inf")


def _compiler_params(semantics):
    return pltpu.CompilerParams(dimension_semantics=semantics,
                                vmem_limit_bytes=VMEM_LIMIT_BYTES)


def _resident(block_shape, index_map):
    return pl.BlockSpec(block_shape, index_map, pipeline_mode=pl.Buffered(1))


def _rope_pair_perm(n_heads):
    idx = []
    for p in range(n_heads // 2):
        a, b = 2 * p * HEAD_DIM, (2 * p + 1) * HEAD_DIM
        for base in (a, b, a + HALF_DIM, b + HALF_DIM):
            idx.extend(range(base, base + HALF_DIM))
    return jnp.asarray(idx, dtype=jnp.int32)


def _rotary_tables(seq):
    inv = 1.0 / (ROPE_THETA ** (jnp.arange(0, HEAD_DIM, 2, dtype=F32) / HEAD_DIM))
    ang = jnp.arange(seq, dtype=F32)[:, None] * inv[None, :]
    return jnp.cos(ang), jnp.sin(ang)


def _pair_tables(cos, sin, scale):
    c = jnp.concatenate([cos, cos, cos, cos], axis=-1) * scale
    s = jnp.concatenate([-sin, -sin, sin, sin], axis=-1) * scale
    return c, s


def _qkv_moba_kernel(x_ref, wk_ref, wqt_ref, wvt_ref, ck_ref, sk_ref, cqt_ref, sqt_ref,
                     k_ref, qt_ref, vt_ref, kmean_ref, *, tm, n_pairs):
    blocks_per_tile = tm // MOBA_BLOCK
    xb = x_ref[0].astype(BF16)

    k = jnp.dot(xb, wk_ref[...], preferred_element_type=F32)
    ck = ck_ref[...]
    sk = sk_ref[...]
    for g in range(n_pairs):
        t = k[:, g * LANES:(g + 1) * LANES]
        kr = t * ck + pltpu.roll(t, HEAD_DIM, axis=1) * sk
        k_ref[0, :, g * LANES:(g + 1) * LANES] = kr.astype(BF16)
        for blk in range(blocks_per_tile):
            mean = jnp.mean(kr[blk * MOBA_BLOCK:(blk + 1) * MOBA_BLOCK], axis=0, keepdims=True)
            kmean_ref[0, 0, blk:blk + 1, g * LANES:(g + 1) * LANES] = mean

    nt_dims = (((1,), (1,)), ((), ()))
    qt = lax.dot_general(wqt_ref[...], xb, nt_dims, preferred_element_type=F32)
    cq = cqt_ref[...]
    sq = sqt_ref[...]
    for g in range(n_pairs):
        t = qt[g * LANES:(g + 1) * LANES]
        swapped = jnp.concatenate([t[HEAD_DIM:], t[:HEAD_DIM]], axis=0)
        qr = (t * cq + swapped * sq).astype(BF16)
        for j in range(blocks_per_tile):
            qt_ref[0, g, j] = qr[:, j * MOBA_BLOCK:(j + 1) * MOBA_BLOCK]

    vt = lax.dot_general(wvt_ref[...], xb, nt_dims, preferred_element_type=F32).astype(BF16)
    for g in range(n_pairs):
        for j in range(blocks_per_tile):
            vt_ref[0, g, j] = vt[g * LANES:(g + 1) * LANES, j * MOBA_BLOCK:(j + 1) * MOBA_BLOCK]


def _qkv_moba(x, wk, wqt, wvt, ck, sk, cqt, sqt, *, tm):
    B, S, D = x.shape
    hd = wk.shape[1]
    n_pairs = hd // LANES
    nb = S // MOBA_BLOCK
    kernel = functools.partial(_qkv_moba_kernel, tm=tm, n_pairs=n_pairs)
    bpt = tm // MOBA_BLOCK
    return pl.pallas_call(
        kernel,
        grid=(B, S // tm),
        in_specs=[
            pl.BlockSpec((1, tm, D), lambda b, t: (b, t, 0)),
            _resident((D, hd), lambda b, t: (0, 0)),
            _resident((hd, D), lambda b, t: (0, 0)),
            _resident((hd, D), lambda b, t: (0, 0)),
            pl.BlockSpec((tm, LANES), lambda b, t: (t, 0)),
            pl.BlockSpec((tm, LANES), lambda b, t: (t, 0)),
            pl.BlockSpec((LANES, tm), lambda b, t: (0, t)),
            pl.BlockSpec((LANES, tm), lambda b, t: (0, t)),
        ],
        out_specs=[
            pl.BlockSpec((1, tm, hd), lambda b, t: (b, t, 0)),
            pl.BlockSpec((1, n_pairs, bpt, LANES, MOBA_BLOCK), lambda b, t: (b, 0, t, 0, 0)),
            pl.BlockSpec((1, n_pairs, bpt, LANES, MOBA_BLOCK), lambda b, t: (b, 0, t, 0, 0)),
            pl.BlockSpec((1, 1, bpt, hd), lambda b, t: (b, t, 0, 0)),
        ],
        out_shape=[
            jax.ShapeDtypeStruct((B, S, hd), BF16),
            jax.ShapeDtypeStruct((B, n_pairs, nb, LANES, MOBA_BLOCK), BF16),
            jax.ShapeDtypeStruct((B, n_pairs, nb, LANES, MOBA_BLOCK), BF16),
            jax.ShapeDtypeStruct((B, S // tm, bpt, hd), F32),
        ],
        compiler_params=_compiler_params(("parallel", "parallel")),
        name="qkv_moba",
    )(x, wk, wqt, wvt, ck, sk, cqt, sqt)


def _top3_blocks(gate, n_past):
    nb = gate.shape[0]
    row = lax.broadcasted_iota(jnp.int32, gate.shape, 0).astype(F32)
    g = jnp.where(row < n_past.astype(F32), gate, NEG_INF)
    sel = jnp.zeros(gate.shape, F32)
    for _ in range(MOBA_TOPK):
        mx = jnp.max(g, axis=0, keepdims=True)
        is_mx = jnp.logical_and(g == mx, g > NEG_INF)
        first = jnp.min(jnp.where(is_mx, row, nb), axis=0, keepdims=True)
        pick = row == first
        sel = jnp.where(pick, 1.0, sel)
        g = jnp.where(pick, NEG_INF, g)
    return sel


def _moba_kernel(qt_ref, k_ref, vt_ref, kmean_ref, o_ref, sel_ref, acc_ref, m_ref, *, nb, unroll):
    blk = MOBA_BLOCK
    kmean = kmean_ref[0].astype(BF16)
    zeros_half = jnp.zeros((HALF_DIM, blk), BF16)
    ones_rows = jnp.ones((BF16_SUBLANES, blk), BF16)
    key_idx = lax.broadcasted_iota(jnp.int32, (blk, blk), 0)
    qry_idx = lax.broadcasted_iota(jnp.int32, (blk, blk), 1)
    causal = key_idx <= qry_idx

    def values_aug(vt_pair, h):
        return jnp.concatenate([vt_pair[h * HEAD_DIM:(h + 1) * HEAD_DIM], ones_rows], axis=0)

    def q_tile(i, carry):
        qt = qt_ref[0, 0, i]
        q_heads = (
            jnp.concatenate([qt[0:32], zeros_half, qt[64:96], zeros_half], axis=0),
            jnp.concatenate([zeros_half, qt[32:64], zeros_half, qt[96:128]], axis=0),
        )
        k_own = k_ref[0, pl.ds(pl.multiple_of(i * blk, blk), blk), :]
        vt_own = vt_ref[0, 0, i]

        for h in range(2):
            gate = jnp.dot(kmean, q_heads[h], preferred_element_type=F32)
            sel_ref[h] = _top3_blocks(gate, i)
            s = jnp.dot(k_own, q_heads[h], preferred_element_type=F32)
            s = jnp.where(causal, s, NEG_INF)
            m = jnp.max(s, axis=0, keepdims=True)
            p = jnp.exp2(s - m).astype(BF16)
            acc_ref[h] = jnp.dot(values_aug(vt_own, h), p, preferred_element_type=F32)
            m_ref[h] = m

        def past_group(g, c):
            for h in range(2):
                m_old = m_ref[h]
                m_new = m_old
                scores = []
                for u in range(unroll):
                    j = g * unroll + u
                    k_j = k_ref[0, pl.ds(pl.multiple_of(j * blk, blk), blk), :]
                    s = jnp.dot(k_j, q_heads[h], preferred_element_type=F32)
                    chosen = sel_ref[h, pl.ds(j, 1), :] > 0.0
                    bmax = jnp.max(s, axis=0, keepdims=True)
                    m_new = jnp.maximum(m_new, jnp.where(chosen, bmax, NEG_INF))
                    scores.append((s, chosen))
                probs = []
                vals = []
                for u in range(unroll):
                    s, chosen = scores[u]
                    shift = jnp.where(chosen, m_new, POS_INF)
                    probs.append(jnp.exp2(s - shift).astype(BF16))
                    vals.append(values_aug(vt_ref[0, 0, g * unroll + u], h))
                pv = jnp.dot(jnp.concatenate(vals, axis=1), jnp.concatenate(probs, axis=0),
                             preferred_element_type=F32)
                acc_ref[h] = acc_ref[h] * jnp.exp2(m_old - m_new) + pv
                m_ref[h] = m_new
            return c

        lax.fori_loop(0, (i + unroll - 1) // unroll, past_group, 0)

        outs = []
        for h in range(2):
            acc = acc_ref[h]
            outs.append(acc[:HEAD_DIM] * (1.0 / acc[HEAD_DIM:HEAD_DIM + 1]))
        o_pair = jnp.concatenate(outs, axis=0)
        o_ref[0, pl.ds(pl.multiple_of(i * blk, blk), blk), :] = o_pair.T.astype(BF16)
        return carry

    lax.fori_loop(0, nb, q_tile, 0)


def _moba_attention(qt, k, vt, kmean, *, unroll):
    B, n_pairs, nb, _, blk = qt.shape
    S = k.shape[1]
    hd = k.shape[2]
    assert nb % unroll == 0
    kernel = functools.partial(_moba_kernel, nb=nb, unroll=unroll)
    return pl.pallas_call(
        kernel,
        grid=(B, n_pairs),
        in_specs=[
            pl.BlockSpec((1, 1, nb, LANES, blk), lambda b, p: (b, p, 0, 0, 0)),
            pl.BlockSpec((1, S, LANES), lambda b, p: (b, 0, p)),
            pl.BlockSpec((1, 1, nb, LANES, blk), lambda b, p: (b, p, 0, 0, 0)),
            pl.BlockSpec((1, nb, LANES), lambda b, p: (b, 0, p)),
        ],
        out_specs=pl.BlockSpec((1, S, LANES), lambda b, p: (b, 0, p)),
        out_shape=jax.ShapeDtypeStruct((B, S, hd), BF16),
        scratch_shapes=[
            pltpu.VMEM((2, nb, blk), F32),
            pltpu.VMEM((2, HEAD_DIM + BF16_SUBLANES, blk), F32),
            pltpu.VMEM((2, 1, blk), F32),
        ],
        compiler_params=_compiler_params(("parallel", "parallel")),
        name="moba_attention",
    )(qt, k, vt, kmean)


def _qkv_dil_kernel(x_ref, w_ref, cq_ref, sq_ref, ck_ref, sk_ref, q_ref, k_ref, v_ref, *, hd):
    xb = x_ref[...].astype(BF16)
    qkv = jnp.dot(xb, w_ref[...], preferred_element_type=F32)
    for g in range(hd // LANES):
        lo, hi = g * LANES, (g + 1) * LANES
        t = qkv[:, lo:hi]
        q_ref[:, lo:hi] = (t * cq_ref[...] + pltpu.roll(t, HEAD_DIM, axis=1) * sq_ref[...]).astype(BF16)
        t = qkv[:, hd + lo:hd + hi]
        k_ref[:, lo:hi] = (t * ck_ref[...] + pltpu.roll(t, HEAD_DIM, axis=1) * sk_ref[...]).astype(BF16)
    v_ref[...] = qkv[:, 2 * hd:].astype(BF16)


def _qkv_dil(x2d, w, cq, sq, ck, sk, *, tm, seq):
    n, d = x2d.shape
    hd = w.shape[1] // 3
    tiles_per_seq = seq // tm
    kernel = functools.partial(_qkv_dil_kernel, hd=hd)
    tab = pl.BlockSpec((tm, LANES), lambda t: (t % tiles_per_seq, 0))
    out = pl.BlockSpec((tm, hd), lambda t: (t, 0))
    return pl.pallas_call(
        kernel,
        grid=(n // tm,),
        in_specs=[pl.BlockSpec((tm, d), lambda t: (t, 0)),
                  _resident((d, 3 * hd), lambda t: (0, 0)),
                  tab, tab, tab, tab],
        out_specs=[out, out, out],
        out_shape=[jax.ShapeDtypeStruct((n, hd), BF16)] * 3,
        compiler_params=_compiler_params(("parallel",)),
        name="qkv_dilated",
    )(x2d, w, cq, sq, ck, sk)


def _dilated_kernel(q_ref, kc_ref, kh_ref, vc_ref, vh_ref, o_ref, lse_ref, *, tl, span):
    wb = DIL_BLOCK
    n = pl.program_id(2)
    q = q_ref[0]
    k_all = jnp.concatenate([kh_ref[0], kc_ref[0]], axis=0)
    v_all = jnp.concatenate([vh_ref[0], vc_ref[0]], axis=0)
    lane = lax.broadcasted_iota(jnp.int32, (wb, LANES), 1)
    first_head_lanes = (lane % HEAD_DIM) < HALF_DIM
    first_head_out = lane < HEAD_DIM
    qq = lax.broadcasted_iota(jnp.int32, (wb, 2 * wb), 0)
    kk = lax.broadcasted_iota(jnp.int32, (wb, 2 * wb), 1)
    dist = qq + wb - kk
    band = jnp.logical_and(dist >= 0, dist <= span)
    nt_dims = (((1,), (1,)), ((), ()))
    for pair in range(q.shape[1] // LANES):
        lo, hi = pair * LANES, (pair + 1) * LANES
        for sub in range(tl // wb):
            qs = q[sub * wb:(sub + 1) * wb, lo:hi]
            ks = k_all[sub * wb:sub * wb + 2 * wb, lo:hi]
            vs = v_all[sub * wb:sub * wb + 2 * wb, lo:hi]
            if sub == 0:
                mask = jnp.logical_and(band, jnp.logical_or(kk >= wb, n > 0))
            else:
                mask = band
            outs, lses = [], []
            for h in range(2):
                keep = first_head_lanes if h == 0 else jnp.logical_not(first_head_lanes)
                qm = jnp.where(keep, qs, jnp.zeros_like(qs))
                s = lax.dot_general(qm, ks, nt_dims, preferred_element_type=F32)
                s = jnp.where(mask, s, NEG_INF)
                m = jnp.max(s, axis=1, keepdims=True)
                p = jnp.exp2(s - m)
                l = jnp.sum(p, axis=1, keepdims=True)
                pv = jnp.dot(p.astype(BF16), vs, preferred_element_type=F32)
                outs.append(pv * (1.0 / l))
                lses.append(m + jnp.log2(l))
            o_ref[0, sub * wb:(sub + 1) * wb, lo:hi] = jnp.where(first_head_out, outs[0], outs[1]).astype(BF16)
            lse_ref[0, sub * wb:(sub + 1) * wb, lo:hi] = jnp.where(
                first_head_out, jnp.broadcast_to(lses[0], (wb, LANES)), jnp.broadcast_to(lses[1], (wb, LANES)))


def _dilated_group(q, k, v, *, group, window, dilation, batch, seq):
    hd = q.shape[1]
    gw = DIL_HEADS_PER_GROUP * HEAD_DIM
    L = seq // dilation
    span = window // dilation
    assert span <= DIL_BLOCK and L % DIL_BLOCK == 0
    tl = min(4 * DIL_BLOCK, L)
    halo_per_tile = tl // DIL_BLOCK
    view = lambda a: a.reshape(batch, L, dilation * a.shape[-1])
    groups_per_pos = hd // gw
    cur = pl.BlockSpec((1, tl, gw), lambda b, r, n: (b, n, r * groups_per_pos + group))
    halo = pl.BlockSpec((1, DIL_BLOCK, gw),
                        lambda b, r, n: (b, jnp.maximum(n * halo_per_tile - 1, 0), r * groups_per_pos + group))
    out = pl.BlockSpec((1, tl, gw), lambda b, r, n: (b, n, r))
    kernel = functools.partial(_dilated_kernel, tl=tl, span=span)
    o, lse = pl.pallas_call(
        kernel,
        grid=(batch, dilation, L // tl),
        in_specs=[cur, cur, halo, cur, halo],
        out_specs=[out, out],
        out_shape=[jax.ShapeDtypeStruct((batch, L, dilation * gw), BF16),
                   jax.ShapeDtypeStruct((batch, L, dilation * gw), F32)],
        compiler_params=_compiler_params(("parallel", "parallel", "arbitrary")),
        name=f"dilated_attention_g{group}",
    )(view(q), view(k), view(k), view(v), view(v))
    return o.reshape(batch * seq, gw), lse.reshape(batch * seq, gw)


def _layer_norm(y, g, b):
    mu = jnp.mean(y, axis=-1, keepdims=True)
    yc = y - mu
    var = jnp.mean(yc * yc, axis=-1, keepdims=True)
    return yc * lax.rsqrt(var + LN_EPS) * g + b


def _tail_kernel(*refs, n_groups, ff_chunk):
    mix_refs = refs[:2 * n_groups] if n_groups > 1 else refs[:1]
    (x_ref, wo_ref, g1_ref, b1_ref, win_ref, wout_ref, g2_ref, b2_ref, out_ref) = refs[len(mix_refs):]
    if n_groups == 1:
        o = mix_refs[0][...]
    else:
        lses = [r[...] for r in mix_refs[n_groups:]]
        top = functools.reduce(jnp.maximum, lses)
        ws = [jnp.exp2(l - top) for l in lses]
        inv = 1.0 / functools.reduce(jnp.add, ws)
        o = jnp.concatenate(
            [(mix_refs[g][...].astype(F32) * (ws[g] * inv)).astype(BF16) for g in range(n_groups)], axis=1)
    mix = jnp.dot(o, wo_ref[...], preferred_element_type=F32)
    h = _layer_norm(DEEPNORM_ALPHA * x_ref[...] + mix, g1_ref[...], b1_ref[...])
    hb = h.astype(BF16)
    d_ff = win_ref.shape[1]
    mlp = jnp.zeros_like(h)
    for c in range(d_ff // ff_chunk):
        lo, hi = c * ff_chunk, (c + 1) * ff_chunk
        hid = jnp.maximum(jnp.dot(hb, win_ref[:, lo:hi], preferred_element_type=F32), 0.0)
        mlp = mlp + jnp.dot((hid * hid).astype(BF16), wout_ref[lo:hi, :], preferred_element_type=F32)
    out_ref[...] = _layer_norm(DEEPNORM_ALPHA * h + mlp, g2_ref[...], b2_ref[...])


def _layer_tail(mix_inputs, x2d, wo, g1, b1, win, wout, g2, b2, *, tm, n_groups):
    n, d = x2d.shape
    d_ff = win.shape[1]
    row = lambda a: pl.BlockSpec((tm, a.shape[1]), lambda t: (t, 0))
    vec = _resident((1, d), lambda t: (0, 0))
    kernel = functools.partial(_tail_kernel, n_groups=n_groups, ff_chunk=min(1024, d_ff))
    return pl.pallas_call(
        kernel,
        grid=(n // tm,),
        in_specs=[row(a) for a in mix_inputs] + [
            row(x2d),
            _resident(wo.shape, lambda t: (0, 0)),
            vec, vec,
            _resident(win.shape, lambda t: (0, 0)),
            _resident(wout.shape, lambda t: (0, 0)),
            vec, vec,
        ],
        out_specs=pl.BlockSpec((tm, d), lambda t: (t, 0)),
        out_shape=jax.ShapeDtypeStruct((n, d), F32),
        compiler_params=_compiler_params(("parallel",)),
        name=f"layer_tail_{n_groups}",
    )(*mix_inputs, x2d, wo, g1, b1, win, wout, g2, b2)


def kernel(x, moba_w_qkv, moba_w_o, dil_w_qkv, dil_w_o, mlp_w_in, mlp_w_out,
           ln_mix_g, ln_mix_b, ln_mlp_g, ln_mlp_b):
    B, S, D = x.shape
    n_tok = B * S
    tm = min(512, S)
    q_scale = HEAD_DIM ** -0.5 * LOG2E
    cos, sin = _rotary_tables(S)
    cq, sq = _pair_tables(cos, sin, q_scale)
    ck, sk = _pair_tables(cos, sin, 1.0)
    vec = lambda a: a.reshape(1, D).astype(F32)

    w = moba_w_qkv[0]
    hd = w.shape[1] // 3
    perm = _rope_pair_perm(hd // HEAD_DIM)
    wq = w[:, :hd][:, perm]
    wk = w[:, hd:2 * hd][:, perm]
    wv = w[:, 2 * hd:]
    k, qt, vt, kmean = _qkv_moba(x, wk.astype(BF16), wq.T.astype(BF16), wv.T.astype(BF16),
                                 ck, sk, cq.T, sq.T, tm=tm)
    attn = _moba_attention(qt, k, vt, kmean.reshape(B, S // MOBA_BLOCK, hd), unroll=4)
    h = _layer_tail([attn.reshape(n_tok, hd)], x.reshape(n_tok, D), moba_w_o[0].astype(BF16),
                    vec(ln_mix_g[0]), vec(ln_mix_b[0]), mlp_w_in[0].astype(BF16), mlp_w_out[0].astype(BF16),
                    vec(ln_mlp_g[0]), vec(ln_mlp_b[0]), tm=tm, n_groups=1)

    w = dil_w_qkv[0]
    hd = w.shape[1] // 3
    perm = _rope_pair_perm(hd // HEAD_DIM)
    w_all = jnp.concatenate([w[:, :hd][:, perm], w[:, hd:2 * hd][:, perm], w[:, 2 * hd:]], axis=1)
    q, kd, v = _qkv_dil(h, w_all.astype(BF16), cq, sq, ck, sk, tm=tm, seq=S)
    outs, lses = [], []
    for g, (window, dilation) in enumerate(DIL_GROUPS):
        o_g, lse_g = _dilated_group(q, kd, v, group=g, window=window, dilation=dilation, batch=B, seq=S)
        outs.append(o_g)
        lses.append(lse_g)
    h = _layer_tail(outs + lses, h, dil_w_o[0].astype(BF16),
                    vec(ln_mix_g[1]), vec(ln_mix_b[1]), mlp_w_in[1].astype(BF16), mlp_w_out[1].astype(BF16),
                    vec(ln_mlp_g[1]), vec(ln_mlp_b[1]), tm=tm, n_groups=len(DIL_GROUPS))
    return h.reshape(B, S, D)
```

```python
import functools
import math

import jax
import jax.numpy as jnp
from jax import lax
from jax.experimental import pallas as pl
from jax.experimental.pallas import tpu as pltpu

HEAD_DIM = 64
HALF_DIM = HEAD_DIM // 2
ROPE_THETA = 10000.0
LN_EPS = 1e-5
DEPTH = 2
DEEPNORM_ALPHA = (2.0 * DEPTH) ** 0.25
MOBA_BLOCK = 256
MOBA_TOPK = 3
DIL_GROUPS = ((128, 1), (512, 4), (2048, 16))
DIL_HEADS_PER_GROUP = 4
DIL_BLOCK = 128

LANES = 128
BF16_SUBLANES = 16
VMEM_LIMIT_BYTES = 56 * 1024 * 1024

LOG2E = math.log2(math.e)

F32 = jnp.float32
BF16 = jnp.bfloat16
NEG_INF = float("-inf")
POS_INF = float("inf")


def _compiler_params(semantics):
    return pltpu.CompilerParams(dimension_semantics=semantics,
                                vmem_limit_bytes=VMEM_LIMIT_BYTES)


def _resident(block_shape, index_map):
    return pl.BlockSpec(block_shape, index_map, pipeline_mode=pl.Buffered(1))


def _rope_pair_perm(n_heads):
    idx = []
    for p in range(n_heads // 2):
        a, b = 2 * p * HEAD_DIM, (2 * p + 1) * HEAD_DIM
        for base in (a, b, a + HALF_DIM, b + HALF_DIM):
            idx.extend(range(base, base + HALF_DIM))
    return jnp.asarray(idx, dtype=jnp.int32)


def _rotary_tables(seq):
    inv = 1.0 / (ROPE_THETA ** (jnp.arange(0, HEAD_DIM, 2, dtype=F32) / HEAD_DIM))
    ang = jnp.arange(seq, dtype=F32)[:, None] * inv[None, :]
    return jnp.cos(ang), jnp.sin(ang)


def _pair_tables(cos, sin, scale):
    c = jnp.concatenate([cos, cos, cos, cos], axis=-1) * scale
    s = jnp.concatenate([-sin, -sin, sin, sin], axis=-1) * scale
    return c, s


def _qkv_moba_kernel(x_ref, wk_ref, wqt_ref, wvt_ref, ck_ref, sk_ref, cqt_ref, sqt_ref,
                     k_ref, qt_ref, vt_ref, kmean_ref, *, tm, n_pairs):
    blocks_per_tile = tm // MOBA_BLOCK
    xb = x_ref[0].astype(BF16)

    k = jnp.dot(xb, wk_ref[...], preferred_element_type=F32)
    ck = ck_ref[...]
    sk = sk_ref[...]
    for g in range(n_pairs):
        t = k[:, g * LANES:(g + 1) * LANES]
        kr = t * ck + pltpu.roll(t, HEAD_DIM, axis=1) * sk
        k_ref[0, :, g * LANES:(g + 1) * LANES] = kr.astype(BF16)
        for blk in range(blocks_per_tile):
            mean = jnp.mean(kr[blk * MOBA_BLOCK:(blk + 1) * MOBA_BLOCK], axis=0, keepdims=True)
            kmean_ref[0, 0, blk:blk + 1, g * LANES:(g + 1) * LANES] = mean

    nt_dims = (((1,), (1,)), ((), ()))
    qt = lax.dot_general(wqt_ref[...], xb, nt_dims, preferred_element_type=F32)
    cq = cqt_ref[...]
    sq = sqt_ref[...]
    for g in range(n_pairs):
        t = qt[g * LANES:(g + 1) * LANES]
        swapped = jnp.concatenate([t[HEAD_DIM:], t[:HEAD_DIM]], axis=0)
        qr = (t * cq + swapped * sq).astype(BF16)
        for j in range(blocks_per_tile):
            qt_ref[0, g, j] = qr[:, j * MOBA_BLOCK:(j + 1) * MOBA_BLOCK]

    vt = lax.dot_general(wvt_ref[...], xb, nt_dims, preferred_element_type=F32).astype(BF16)
    for g in range(n_pairs):
        for j in range(blocks_per_tile):
            vt_ref[0, g, j] = vt[g * LANES:(g + 1) * LANES, j * MOBA_BLOCK:(j + 1) * MOBA_BLOCK]


def _qkv_moba(x, wk, wqt, wvt, ck, sk, cqt, sqt, *, tm):
    B, S, D = x.shape
    hd = wk.shape[1]
    n_pairs = hd // LANES
    nb = S // MOBA_BLOCK
    kernel = functools.partial(_qkv_moba_kernel, tm=tm, n_pairs=n_pairs)
    bpt = tm // MOBA_BLOCK
    return pl.pallas_call(
        kernel,
        grid=(B, S // tm),
        in_specs=[
            pl.BlockSpec((1, tm, D), lambda b, t: (b, t, 0)),
            _resident((D, hd), lambda b, t: (0, 0)),
            _resident((hd, D), lambda b, t: (0, 0)),
            _resident((hd, D), lambda b, t: (0, 0)),
            pl.BlockSpec((tm, LANES), lambda b, t: (t, 0)),
            pl.BlockSpec((tm, LANES), lambda b, t: (t, 0)),
            pl.BlockSpec((LANES, tm), lambda b, t: (0, t)),
            pl.BlockSpec((LANES, tm), lambda b, t: (0, t)),
        ],
        out_specs=[
            pl.BlockSpec((1, tm, hd), lambda b, t: (b, t, 0)),
            pl.BlockSpec((1, n_pairs, bpt, LANES, MOBA_BLOCK), lambda b, t: (b, 0, t, 0, 0)),
            pl.BlockSpec((1, n_pairs, bpt, LANES, MOBA_BLOCK), lambda b, t: (b, 0, t, 0, 0)),
            pl.BlockSpec((1, 1, bpt, hd), lambda b, t: (b, t, 0, 0)),
        ],
        out_shape=[
            jax.ShapeDtypeStruct((B, S, hd), BF16),
            jax.ShapeDtypeStruct((B, n_pairs, nb, LANES, MOBA_BLOCK), BF16),
            jax.ShapeDtypeStruct((B, n_pairs, nb, LANES, MOBA_BLOCK), BF16),
            jax.ShapeDtypeStruct((B, S // tm, bpt, hd), F32),
        ],
        compiler_params=_compiler_params(("parallel", "parallel")),
        name="qkv_moba",
    )(x, wk, wqt, wvt, ck, sk, cqt, sqt)


def _top3_blocks(gate, n_past):
    nb = gate.shape[0]
    row = lax.broadcasted_iota(jnp.int32, gate.shape, 0).astype(F32)
    g = jnp.where(row < n_past.astype(F32), gate, NEG_INF)
    sel = jnp.zeros(gate.shape, F32)
    for _ in range(MOBA_TOPK):
        mx = jnp.max(g, axis=0, keepdims=True)
        is_mx = jnp.logical_and(g == mx, g > NEG_INF)
        first = jnp.min(jnp.where(is_mx, row, nb), axis=0, keepdims=True)
        pick = row == first
        sel = jnp.where(pick, 1.0, sel)
        g = jnp.where(pick, NEG_INF, g)
    return sel


def _moba_kernel(qt_ref, k_ref, vt_ref, kmean_ref, o_ref, sel_ref, acc_ref, m_ref, s_ref, *, nb, unroll):
    blk = MOBA_BLOCK
    kmean = kmean_ref[0].astype(BF16)
    zeros_half = jnp.zeros((HALF_DIM, blk), BF16)
    ones_rows = jnp.ones((BF16_SUBLANES, blk), BF16)
    key_idx = lax.broadcasted_iota(jnp.int32, (blk, blk), 0)
    qry_idx = lax.broadcasted_iota(jnp.int32, (blk, blk), 1)
    causal = key_idx <= qry_idx

    def values_aug(vt_pair, h):
        return jnp.concatenate([vt_pair[h * HEAD_DIM:(h + 1) * HEAD_DIM], ones_rows], axis=0)

    def q_tile(i, carry):
        qt = qt_ref[0, 0, i]
        q_heads = (
            jnp.concatenate([qt[0:32], zeros_half, qt[64:96], zeros_half], axis=0),
            jnp.concatenate([zeros_half, qt[32:64], zeros_half, qt[96:128]], axis=0),
        )
        k_own = k_ref[0, pl.ds(pl.multiple_of(i * blk, blk), blk), :]
        vt_own = vt_ref[0, 0, i]

        chains = [(h, u) for h in range(2) for u in range(unroll)]

        def group_scores(g):
            out = []
            for h, u in chains:
                j = g * unroll + u
                k_j = k_ref[0, pl.ds(pl.multiple_of(j * blk, blk), blk), :]
                out.append(jnp.dot(k_j, q_heads[h], preferred_element_type=F32))
            return out

        def chain_softmax(c, s, g):
            h, u = chains[c]
            chosen = sel_ref[h, pl.ds(g * unroll + u, 1), :] > 0.0
            m_old = m_ref[h, u]
            m_new = jnp.maximum(m_old, jnp.where(chosen, jnp.max(s, axis=0, keepdims=True), NEG_INF))
            m_ref[h, u] = m_new
            p = jnp.exp2(s - jnp.where(chosen, m_new, POS_INF)).astype(BF16)
            return p, jnp.exp2(m_old - m_new)

        def chain_values(c, p, rescale, g):
            h, u = chains[c]
            pv = jnp.dot(values_aug(vt_ref[0, 0, g * unroll + u], h), p, preferred_element_type=F32)
            acc_ref[h, u] = acc_ref[h, u] * rescale + pv

        gates = [jnp.dot(kmean, q_heads[h], preferred_element_type=F32) for h in range(2)]
        own = [jnp.dot(k_own, q_heads[h], preferred_element_type=F32) for h in range(2)]
        first = group_scores(0)
        own_p = []
        for h in range(2):
            sel_ref[h] = _top3_blocks(gates[h], i)
            s = jnp.where(causal, own[h], NEG_INF)
            m = jnp.max(s, axis=0, keepdims=True)
            own_p.append(jnp.exp2(s - m).astype(BF16))
            for u in range(unroll):
                m_ref[h, u] = m
        for c in range(len(chains)):
            s_ref[c] = first[c]
        for h in range(2):
            acc_ref[h, 0] = jnp.dot(values_aug(vt_own, h), own_p[h], preferred_element_type=F32)
            for u in range(1, unroll):
                acc_ref[h, u] = jnp.zeros(acc_ref.shape[2:], F32)

        n_groups = jnp.maximum((i + unroll - 1) // unroll, 1)

        def past_group(g, carry_):
            ahead = group_scores(g + 1)
            probs = []
            for c in range(len(chains)):
                probs.append(chain_softmax(c, s_ref[c], g))
                s_ref[c] = ahead[c]
            for c in range(len(chains)):
                chain_values(c, *probs[c], g)
            return carry_

        lax.fori_loop(0, n_groups - 1, past_group, 0)

        last = n_groups - 1
        probs = [chain_softmax(c, s_ref[c], last) for c in range(len(chains))]
        for c in range(len(chains)):
            chain_values(c, *probs[c], last)

        outs = []
        for h in range(2):
            m_all = functools.reduce(jnp.maximum, [m_ref[h, u] for u in range(unroll)])
            acc = functools.reduce(
                jnp.add, [acc_ref[h, u] * jnp.exp2(m_ref[h, u] - m_all) for u in range(unroll)])
            outs.append(acc[:HEAD_DIM] * (1.0 / acc[HEAD_DIM:HEAD_DIM + 1]))
        o_pair = jnp.concatenate(outs, axis=0)
        o_ref[0, pl.ds(pl.multiple_of(i * blk, blk), blk), :] = o_pair.T.astype(BF16)
        return carry

    lax.fori_loop(0, nb, q_tile, 0)


def _moba_attention(qt, k, vt, kmean, *, unroll):
    B, n_pairs, nb, _, blk = qt.shape
    S = k.shape[1]
    hd = k.shape[2]
    assert nb % unroll == 0
    kernel = functools.partial(_moba_kernel, nb=nb, unroll=unroll)
    return pl.pallas_call(
        kernel,
        grid=(B, n_pairs),
        in_specs=[
            pl.BlockSpec((1, 1, nb, LANES, blk), lambda b, p: (b, p, 0, 0, 0)),
            pl.BlockSpec((1, S, LANES), lambda b, p: (b, 0, p)),
            pl.BlockSpec((1, 1, nb, LANES, blk), lambda b, p: (b, p, 0, 0, 0)),
            pl.BlockSpec((1, nb, LANES), lambda b, p: (b, 0, p)),
        ],
        out_specs=pl.BlockSpec((1, S, LANES), lambda b, p: (b, 0, p)),
        out_shape=jax.ShapeDtypeStruct((B, S, hd), BF16),
        scratch_shapes=[
            pltpu.VMEM((2, nb, blk), F32),
            pltpu.VMEM((2, unroll, HEAD_DIM + BF16_SUBLANES, blk), F32),
            pltpu.VMEM((2, unroll, 1, blk), F32),
            pltpu.VMEM((2 * unroll, blk, blk), F32),
        ],
        compiler_params=_compiler_params(("parallel", "parallel")),
        name="moba_attention",
    )(qt, k, vt, kmean)


def _qkv_dil_kernel(x_ref, w_ref, cq_ref, sq_ref, ck_ref, sk_ref, q_ref, k_ref, v_ref, *, hd):
    xb = x_ref[...].astype(BF16)
    qkv = jnp.dot(xb, w_ref[...], preferred_element_type=F32)
    for g in range(hd // LANES):
        lo, hi = g * LANES, (g + 1) * LANES
        t = qkv[:, lo:hi]
        q_ref[:, lo:hi] = (t * cq_ref[...] + pltpu.roll(t, HEAD_DIM, axis=1) * sq_ref[...]).astype(BF16)
        t = qkv[:, hd + lo:hd + hi]
        k_ref[:, lo:hi] = (t * ck_ref[...] + pltpu.roll(t, HEAD_DIM, axis=1) * sk_ref[...]).astype(BF16)
    v_ref[...] = qkv[:, 2 * hd:].astype(BF16)


def _qkv_dil(x2d, w, cq, sq, ck, sk, *, tm, seq):
    n, d = x2d.shape
    hd = w.shape[1] // 3
    tiles_per_seq = seq // tm
    kernel = functools.partial(_qkv_dil_kernel, hd=hd)
    tab = pl.BlockSpec((tm, LANES), lambda t: (t % tiles_per_seq, 0))
    out = pl.BlockSpec((tm, hd), lambda t: (t, 0))
    return pl.pallas_call(
        kernel,
        grid=(n // tm,),
        in_specs=[pl.BlockSpec((tm, d), lambda t: (t, 0)),
                  _resident((d, 3 * hd), lambda t: (0, 0)),
                  tab, tab, tab, tab],
        out_specs=[out, out, out],
        out_shape=[jax.ShapeDtypeStruct((n, hd), BF16)] * 3,
        compiler_params=_compiler_params(("parallel",)),
        name="qkv_dilated",
    )(x2d, w, cq, sq, ck, sk)


def _dilated_kernel(q_ref, kc_ref, kh_ref, vc_ref, vh_ref, o_ref, lse_ref, *, tl, span):
    wb = DIL_BLOCK
    n = pl.program_id(2)
    q = q_ref[0]
    k_all = jnp.concatenate([kh_ref[0], kc_ref[0]], axis=0)
    v_all = jnp.concatenate([vh_ref[0], vc_ref[0]], axis=0)
    lane = lax.broadcasted_iota(jnp.int32, (wb, LANES), 1)
    first_head_lanes = (lane % HEAD_DIM) < HALF_DIM
    first_head_out = lane < HEAD_DIM
    qq = lax.broadcasted_iota(jnp.int32, (wb, 2 * wb), 0)
    kk = lax.broadcasted_iota(jnp.int32, (wb, 2 * wb), 1)
    dist = qq + wb - kk
    band = jnp.logical_and(dist >= 0, dist <= span)
    nt_dims = (((1,), (1,)), ((), ()))
    for pair in range(q.shape[1] // LANES):
        lo, hi = pair * LANES, (pair + 1) * LANES
        for sub in range(tl // wb):
            qs = q[sub * wb:(sub + 1) * wb, lo:hi]
            ks = k_all[sub * wb:sub * wb + 2 * wb, lo:hi]
            vs = v_all[sub * wb:sub * wb + 2 * wb, lo:hi]
            if sub == 0:
                mask = jnp.logical_and(band, jnp.logical_or(kk >= wb, n > 0))
            else:
                mask = band
            outs, lses = [], []
            for h in range(2):
                keep = first_head_lanes if h == 0 else jnp.logical_not(first_head_lanes)
                qm = jnp.where(keep, qs, jnp.zeros_like(qs))
                s = lax.dot_general(qm, ks, nt_dims, preferred_element_type=F32)
                s = jnp.where(mask, s, NEG_INF)
                m = jnp.max(s, axis=1, keepdims=True)
                p = jnp.exp2(s - m)
                l = jnp.sum(p, axis=1, keepdims=True)
                pv = jnp.dot(p.astype(BF16), vs, preferred_element_type=F32)
                outs.append(pv * (1.0 / l))
                lses.append(m + jnp.log2(l))
            o_ref[0, sub * wb:(sub + 1) * wb, lo:hi] = jnp.where(first_head_out, outs[0], outs[1]).astype(BF16)
            lse_ref[0, sub * wb:(sub + 1) * wb, lo:hi] = jnp.where(
                first_head_out, jnp.broadcast_to(lses[0], (wb, LANES)), jnp.broadcast_to(lses[1], (wb, LANES)))


def _dilated_group(q, k, v, *, group, window, dilation, batch, seq):
    hd = q.shape[1]
    gw = DIL_HEADS_PER_GROUP * HEAD_DIM
    L = seq // dilation
    span = window // dilation
    assert span <= DIL_BLOCK and L % DIL_BLOCK == 0
    tl = min(4 * DIL_BLOCK, L)
    halo_per_tile = tl // DIL_BLOCK
    view = lambda a: a.reshape(batch, L, dilation * a.shape[-1])
    groups_per_pos = hd // gw
    cur = pl.BlockSpec((1, tl, gw), lambda b, r, n: (b, n, r * groups_per_pos + group))
    halo = pl.BlockSpec((1, DIL_BLOCK, gw),
                        lambda b, r, n: (b, jnp.maximum(n * halo_per_tile - 1, 0), r * groups_per_pos + group))
    out = pl.BlockSpec((1, tl, gw), lambda b, r, n: (b, n, r))
    kernel = functools.partial(_dilated_kernel, tl=tl, span=span)
    o, lse = pl.pallas_call(
        kernel,
        grid=(batch, dilation, L // tl),
        in_specs=[cur, cur, halo, cur, halo],
        out_specs=[out, out],
        out_shape=[jax.ShapeDtypeStruct((batch, L, dilation * gw), BF16),
                   jax.ShapeDtypeStruct((batch, L, dilation * gw), F32)],
        compiler_params=_compiler_params(("parallel", "parallel", "arbitrary")),
        name=f"dilated_attention_g{group}",
    )(view(q), view(k), view(k), view(v), view(v))
    return o.reshape(batch * seq, gw), lse.reshape(batch * seq, gw)


def _layer_norm(y, g, b):
    mu = jnp.mean(y, axis=-1, keepdims=True)
    yc = y - mu
    var = jnp.mean(yc * yc, axis=-1, keepdims=True)
    return yc * lax.rsqrt(var + LN_EPS) * g + b


def _tail_kernel(*refs, n_groups, ff_chunk):
    mix_refs = refs[:2 * n_groups] if n_groups > 1 else refs[:1]
    (x_ref, wo_ref, g1_ref, b1_ref, win_ref, wout_ref, g2_ref, b2_ref, out_ref) = refs[len(mix_refs):]
    if n_groups == 1:
        o = mix_refs[0][...]
    else:
        lses = [r[...] for r in mix_refs[n_groups:]]
        top = functools.reduce(jnp.maximum, lses)
        ws = [jnp.exp2(l - top) for l in lses]
        inv = 1.0 / functools.reduce(jnp.add, ws)
        o = jnp.concatenate(
            [(mix_refs[g][...].astype(F32) * (ws[g] * inv)).astype(BF16) for g in range(n_groups)], axis=1)
    mix = jnp.dot(o, wo_ref[...], preferred_element_type=F32)
    h = _layer_norm(DEEPNORM_ALPHA * x_ref[...] + mix, g1_ref[...], b1_ref[...])
    hb = h.astype(BF16)
    d_ff = win_ref.shape[1]
    mlp = jnp.zeros_like(h)
    for c in range(d_ff // ff_chunk):
        lo, hi = c * ff_chunk, (c + 1) * ff_chunk
        hid = jnp.maximum(jnp.dot(hb, win_ref[:, lo:hi], preferred_element_type=F32), 0.0)
        mlp = mlp + jnp.dot((hid * hid).astype(BF16), wout_ref[lo:hi, :], preferred_element_type=F32)
    out_ref[...] = _layer_norm(DEEPNORM_ALPHA * h + mlp, g2_ref[...], b2_ref[...])


def _layer_tail(mix_inputs, x2d, wo, g1, b1, win, wout, g2, b2, *, tm, n_groups):
    n, d = x2d.shape
    d_ff = win.shape[1]
    row = lambda a: pl.BlockSpec((tm, a.shape[1]), lambda t: (t, 0))
    vec = _resident((1, d), lambda t: (0, 0))
    kernel = functools.partial(_tail_kernel, n_groups=n_groups, ff_chunk=min(1024, d_ff))
    return pl.pallas_call(
        kernel,
        grid=(n // tm,),
        in_specs=[row(a) for a in mix_inputs] + [
            row(x2d),
            _resident(wo.shape, lambda t: (0, 0)),
            vec, vec,
            _resident(win.shape, lambda t: (0, 0)),
            _resident(wout.shape, lambda t: (0, 0)),
            vec, vec,
        ],
        out_specs=pl.BlockSpec((tm, d), lambda t: (t, 0)),
        out_shape=jax.ShapeDtypeStruct((n, d), F32),
        compiler_params=_compiler_params(("parallel",)),
        name=f"layer_tail_{n_groups}",
    )(*mix_inputs, x2d, wo, g1, b1, win, wout, g2, b2)


def kernel(x, moba_w_qkv, moba_w_o, dil_w_qkv, dil_w_o, mlp_w_in, mlp_w_out,
           ln_mix_g, ln_mix_b, ln_mlp_g, ln_mlp_b):
    B, S, D = x.shape
    n_tok = B * S
    tm = min(512, S)
    q_scale = HEAD_DIM ** -0.5 * LOG2E
    cos, sin = _rotary_tables(S)
    cq, sq = _pair_tables(cos, sin, q_scale)
    ck, sk = _pair_tables(cos, sin, 1.0)
    vec = lambda a: a.reshape(1, D).astype(F32)

    w = moba_w_qkv[0]
    hd = w.shape[1] // 3
    perm = _rope_pair_perm(hd // HEAD_DIM)
    wq = w[:, :hd][:, perm]
    wk = w[:, hd:2 * hd][:, perm]
    wv = w[:, 2 * hd:]
    k, qt, vt, kmean = _qkv_moba(x, wk.astype(BF16), wq.T.astype(BF16), wv.T.astype(BF16),
                                 ck, sk, cq.T, sq.T, tm=tm)
    attn = _moba_attention(qt, k, vt, kmean.reshape(B, S // MOBA_BLOCK, hd), unroll=4)
    h = _layer_tail([attn.reshape(n_tok, hd)], x.reshape(n_tok, D), moba_w_o[0].astype(BF16),
                    vec(ln_mix_g[0]), vec(ln_mix_b[0]), mlp_w_in[0].astype(BF16), mlp_w_out[0].astype(BF16),
                    vec(ln_mlp_g[0]), vec(ln_mlp_b[0]), tm=tm, n_groups=1)

    w = dil_w_qkv[0]
    hd = w.shape[1] // 3
    perm = _rope_pair_perm(hd // HEAD_DIM)
    w_all = jnp.concatenate([w[:, :hd][:, perm], w[:, hd:2 * hd][:, perm], w[:, 2 * hd:]], axis=1)
    q, kd, v = _qkv_dil(h, w_all.astype(BF16), cq, sq, ck, sk, tm=tm, seq=S)
    outs, lses = [], []
    for g, (window, dilation) in enumerate(DIL_GROUPS):
        o_g, lse_g = _dilated_group(q, kd, v, group=g, window=window, dilation=dilation, batch=B, seq=S)
        outs.append(o_g)
        lses.append(lse_g)
    h = _layer_tail(outs + lses, h, dil_w_o[0].astype(BF16),
                    vec(ln_mix_g[1]), vec(ln_mix_b[1]), mlp_w_in[1].astype(BF16), mlp_w_out[1].astype(BF16),
                    vec(ln_mlp_g[1]), vec(ln_mlp_b[1]), tm=tm, n_groups=len(DIL_GROUPS))
    return h.reshape(B, S, D)
```

```python
import functools
import math

import jax
import jax.numpy as jnp
from jax import lax
from jax.experimental import pallas as pl
from jax.experimental.pallas import tpu as pltpu

HEAD_DIM = 64
HALF_DIM = HEAD_DIM // 2
ROPE_THETA = 10000.0
LN_EPS = 1e-5
DEPTH = 2
DEEPNORM_ALPHA = (2.0 * DEPTH) ** 0.25
MOBA_BLOCK = 256
MOBA_TOPK = 3
DIL_GROUPS = ((128, 1), (512, 4), (2048, 16))
DIL_HEADS_PER_GROUP = 4
DIL_BLOCK = 128

LANES = 128
BF16_SUBLANES = 16
VMEM_LIMIT_BYTES = 56 * 1024 * 1024

LOG2E = math.log2(math.e)

F32 = jnp.float32
BF16 = jnp.bfloat16
NEG_INF = float("-inf")
POS_INF = float("inf")


def _compiler_params(semantics):
    return pltpu.CompilerParams(dimension_semantics=semantics,
                                vmem_limit_bytes=VMEM_LIMIT_BYTES)


def _resident(block_shape, index_map):
    return pl.BlockSpec(block_shape, index_map, pipeline_mode=pl.Buffered(1))


def _rope_pair_perm(n_heads):
    idx = []
    for p in range(n_heads // 2):
        a, b = 2 * p * HEAD_DIM, (2 * p + 1) * HEAD_DIM
        for base in (a, b, a + HALF_DIM, b + HALF_DIM):
            idx.extend(range(base, base + HALF_DIM))
    return jnp.asarray(idx, dtype=jnp.int32)


def _rotary_tables(seq):
    inv = 1.0 / (ROPE_THETA ** (jnp.arange(0, HEAD_DIM, 2, dtype=F32) / HEAD_DIM))
    ang = jnp.arange(seq, dtype=F32)[:, None] * inv[None, :]
    return jnp.cos(ang), jnp.sin(ang)


def _pair_tables(cos, sin, scale):
    c = jnp.concatenate([cos, cos, cos, cos], axis=-1) * scale
    s = jnp.concatenate([-sin, -sin, sin, sin], axis=-1) * scale
    return c, s


def _qkv_moba_kernel(x_ref, wk_ref, wqt_ref, wvt_ref, ck_ref, sk_ref, cqt_ref, sqt_ref,
                     k_ref, qt_ref, vt_ref, kmean_ref, *, tm, n_pairs):
    blocks_per_tile = tm // MOBA_BLOCK
    xb = x_ref[0].astype(BF16)

    k = jnp.dot(xb, wk_ref[...], preferred_element_type=F32)
    ck = ck_ref[...]
    sk = sk_ref[...]
    for g in range(n_pairs):
        t = k[:, g * LANES:(g + 1) * LANES]
        kr = t * ck + pltpu.roll(t, HEAD_DIM, axis=1) * sk
        k_ref[0, :, g * LANES:(g + 1) * LANES] = kr.astype(BF16)
        for blk in range(blocks_per_tile):
            mean = jnp.mean(kr[blk * MOBA_BLOCK:(blk + 1) * MOBA_BLOCK], axis=0, keepdims=True)
            kmean_ref[0, 0, blk:blk + 1, g * LANES:(g + 1) * LANES] = mean

    nt_dims = (((1,), (1,)), ((), ()))
    qt = lax.dot_general(wqt_ref[...], xb, nt_dims, preferred_element_type=F32)
    cq = cqt_ref[...]
    sq = sqt_ref[...]
    for g in range(n_pairs):
        t = qt[g * LANES:(g + 1) * LANES]
        swapped = jnp.concatenate([t[HEAD_DIM:], t[:HEAD_DIM]], axis=0)
        qr = (t * cq + swapped * sq).astype(BF16)
        for j in range(blocks_per_tile):
            qt_ref[0, g, j] = qr[:, j * MOBA_BLOCK:(j + 1) * MOBA_BLOCK]

    vt = lax.dot_general(wvt_ref[...], xb, nt_dims, preferred_element_type=F32).astype(BF16)
    for g in range(n_pairs):
        for j in range(blocks_per_tile):
            vt_ref[0, g, j] = vt[g * LANES:(g + 1) * LANES, j * MOBA_BLOCK:(j + 1) * MOBA_BLOCK]


def _qkv_moba(x, wk, wqt, wvt, ck, sk, cqt, sqt, *, tm):
    B, S, D = x.shape
    hd = wk.shape[1]
    n_pairs = hd // LANES
    nb = S // MOBA_BLOCK
    kernel = functools.partial(_qkv_moba_kernel, tm=tm, n_pairs=n_pairs)
    bpt = tm // MOBA_BLOCK
    return pl.pallas_call(
        kernel,
        grid=(B, S // tm),
        in_specs=[
            pl.BlockSpec((1, tm, D), lambda b, t: (b, t, 0)),
            _resident((D, hd), lambda b, t: (0, 0)),
            _resident((hd, D), lambda b, t: (0, 0)),
            _resident((hd, D), lambda b, t: (0, 0)),
            pl.BlockSpec((tm, LANES), lambda b, t: (t, 0)),
            pl.BlockSpec((tm, LANES), lambda b, t: (t, 0)),
            pl.BlockSpec((LANES, tm), lambda b, t: (0, t)),
            pl.BlockSpec((LANES, tm), lambda b, t: (0, t)),
        ],
        out_specs=[
            pl.BlockSpec((1, tm, hd), lambda b, t: (b, t, 0)),
            pl.BlockSpec((1, n_pairs, bpt, LANES, MOBA_BLOCK), lambda b, t: (b, 0, t, 0, 0)),
            pl.BlockSpec((1, n_pairs, bpt, LANES, MOBA_BLOCK), lambda b, t: (b, 0, t, 0, 0)),
            pl.BlockSpec((1, 1, bpt, hd), lambda b, t: (b, t, 0, 0)),
        ],
        out_shape=[
            jax.ShapeDtypeStruct((B, S, hd), BF16),
            jax.ShapeDtypeStruct((B, n_pairs, nb, LANES, MOBA_BLOCK), BF16),
            jax.ShapeDtypeStruct((B, n_pairs, nb, LANES, MOBA_BLOCK), BF16),
            jax.ShapeDtypeStruct((B, S // tm, bpt, hd), F32),
        ],
        compiler_params=_compiler_params(("parallel", "parallel")),
        name="qkv_moba",
    )(x, wk, wqt, wvt, ck, sk, cqt, sqt)


def _top3_blocks(gate, n_past):
    nb = gate.shape[0]
    row_i = lax.broadcasted_iota(jnp.int32, gate.shape, 0)
    row = row_i.astype(F32)
    g = jnp.where(row_i < n_past, gate, NEG_INF)
    sel = jnp.zeros(gate.shape, F32)
    for _ in range(MOBA_TOPK):
        mx = jnp.max(g, axis=0, keepdims=True)
        is_mx = jnp.logical_and(g == mx, g > NEG_INF)
        first = jnp.min(jnp.where(is_mx, row, nb), axis=0, keepdims=True)
        pick = row == first
        sel = jnp.where(pick, 1.0, sel)
        g = jnp.where(pick, NEG_INF, g)
    return sel


def _moba_kernel(qt_ref, k_ref, vt_ref, kmean_ref, o_ref, sel_ref, acc_ref, m_ref, s_ref, *,
                 nb, unroll, pairs):
    blk = MOBA_BLOCK
    n_heads = 2 * pairs
    kmean = kmean_ref[0].astype(BF16)
    zeros_half = jnp.zeros((HALF_DIM, blk), BF16)
    ones_rows = jnp.ones((BF16_SUBLANES, blk), BF16)
    key_idx = lax.broadcasted_iota(jnp.int32, (blk, blk), 0)
    qry_idx = lax.broadcasted_iota(jnp.int32, (blk, blk), 1)
    causal = key_idx <= qry_idx

    def values_aug(j, h):
        vt_pair = vt_ref[0, h // 2, j]
        return jnp.concatenate([vt_pair[(h % 2) * HEAD_DIM:(h % 2 + 1) * HEAD_DIM], ones_rows], axis=0)

    def q_tile(i, carry):
        q_heads = []
        for h in range(n_heads):
            qt = qt_ref[0, h // 2, i]
            lo = (h % 2) * HALF_DIM
            rows = [zeros_half] * (4 * pairs)
            rows[4 * (h // 2) + h % 2] = qt[lo:lo + HALF_DIM]
            rows[4 * (h // 2) + 2 + h % 2] = qt[HEAD_DIM + lo:HEAD_DIM + lo + HALF_DIM]
            q_heads.append(jnp.concatenate(rows, axis=0))
        k_own = k_ref[0, pl.ds(pl.multiple_of(i * blk, blk), blk), :]

        chains = [(h, u) for h in range(n_heads) for u in range(unroll)]

        def group_scores(g):
            out = []
            for h, u in chains:
                j = g * unroll + u
                k_j = k_ref[0, pl.ds(pl.multiple_of(j * blk, blk), blk), :]
                out.append(jnp.dot(k_j, q_heads[h], preferred_element_type=F32))
            return out

        def chain_softmax(c, g):
            h, u = chains[c]
            chosen = sel_ref[h, pl.ds(g * unroll + u, 1), :] > 0.0
            m_old = m_ref[h, u]
            m_new = jnp.maximum(m_old, jnp.where(chosen, jnp.max(s_ref[c], axis=0, keepdims=True), NEG_INF))
            m_ref[h, u] = m_new
            p = jnp.exp2(s_ref[c] - jnp.where(chosen, m_new, POS_INF)).astype(BF16)
            return p, jnp.exp2(m_old - m_new)

        def chain_values(c, p, rescale, g):
            h, u = chains[c]
            pv = jnp.dot(values_aug(g * unroll + u, h), p, preferred_element_type=F32)
            acc_ref[h, u] = acc_ref[h, u] * rescale + pv

        gates = [jnp.dot(kmean, q_heads[h], preferred_element_type=F32) for h in range(n_heads)]
        own = [jnp.dot(k_own, q_heads[h], preferred_element_type=F32) for h in range(n_heads)]
        first = group_scores(0)
        own_p = []
        for h in range(n_heads):
            sel_ref[h] = _top3_blocks(gates[h], i)
            s = jnp.where(causal, own[h], NEG_INF)
            m = jnp.max(s, axis=0, keepdims=True)
            own_p.append(jnp.exp2(s - m).astype(BF16))
            for u in range(unroll):
                m_ref[h, u] = m
        for c in range(len(chains)):
            s_ref[c] = first[c]
        for h in range(n_heads):
            acc_ref[h, 0] = jnp.dot(values_aug(i, h), own_p[h], preferred_element_type=F32)
            for u in range(1, unroll):
                acc_ref[h, u] = jnp.zeros(acc_ref.shape[2:], F32)

        n_groups = jnp.maximum((i + unroll - 1) // unroll, 1)

        def past_group(g, carry_):
            ahead = group_scores(g + 1)
            probs = []
            for c in range(len(chains)):
                probs.append(chain_softmax(c, g))
                s_ref[c] = ahead[c]
            for c in range(len(chains)):
                chain_values(c, *probs[c], g)
            return carry_

        lax.fori_loop(0, n_groups - 1, past_group, 0)

        last = n_groups - 1
        probs = [chain_softmax(c, last) for c in range(len(chains))]
        for c in range(len(chains)):
            chain_values(c, *probs[c], last)

        outs = []
        for h in range(n_heads):
            m_all = functools.reduce(jnp.maximum, [m_ref[h, u] for u in range(unroll)])
            acc = functools.reduce(
                jnp.add, [acc_ref[h, u] * jnp.exp2(m_ref[h, u] - m_all) for u in range(unroll)])
            outs.append(acc[:HEAD_DIM] * (1.0 / acc[HEAD_DIM:HEAD_DIM + 1]))
        o_all = jnp.concatenate(outs, axis=0)
        o_ref[0, pl.ds(pl.multiple_of(i * blk, blk), blk), :] = o_all.T.astype(BF16)
        return carry

    lax.fori_loop(0, nb, q_tile, 0)


def _moba_attention(qt, k, vt, kmean, *, unroll, pairs):
    B, n_pairs, nb, _, blk = qt.shape
    S = k.shape[1]
    hd = k.shape[2]
    assert nb % unroll == 0 and n_pairs % pairs == 0
    n_heads = 2 * pairs
    width = pairs * LANES
    kernel = functools.partial(_moba_kernel, nb=nb, unroll=unroll, pairs=pairs)
    return pl.pallas_call(
        kernel,
        grid=(B, n_pairs // pairs),
        in_specs=[
            pl.BlockSpec((1, pairs, nb, LANES, blk), lambda b, p: (b, p, 0, 0, 0)),
            pl.BlockSpec((1, S, width), lambda b, p: (b, 0, p)),
            pl.BlockSpec((1, pairs, nb, LANES, blk), lambda b, p: (b, p, 0, 0, 0)),
            pl.BlockSpec((1, nb, width), lambda b, p: (b, 0, p)),
        ],
        out_specs=pl.BlockSpec((1, S, width), lambda b, p: (b, 0, p)),
        out_shape=jax.ShapeDtypeStruct((B, S, hd), BF16),
        scratch_shapes=[
            pltpu.VMEM((n_heads, nb, blk), F32),
            pltpu.VMEM((n_heads, unroll, HEAD_DIM + BF16_SUBLANES, blk), F32),
            pltpu.VMEM((n_heads, unroll, 1, blk), F32),
            pltpu.VMEM((n_heads * unroll, blk, blk), F32),
        ],
        compiler_params=_compiler_params(("parallel", "parallel")),
        name="moba_attention",
    )(qt, k, vt, kmean)


def _qkv_dil_kernel(x_ref, w_ref, cq_ref, sq_ref, ck_ref, sk_ref, *refs, hd, dilations, tm):
    out_refs, scr_ref = refs[:-1], refs[-1]
    gw = hd // len(dilations)
    xb = x_ref[...].astype(BF16)
    qkv = jnp.dot(xb, w_ref[...], preferred_element_type=F32)
    slot = 0
    for g, d in enumerate(dilations):
        for kind in range(3):
            ref = out_refs[3 * g + kind]
            for slab in range(gw // LANES):
                col = kind * hd + g * gw + slab * LANES
                t = qkv[:, col:col + LANES]
                if kind == 0:
                    t = t * cq_ref[...] + pltpu.roll(t, HEAD_DIM, axis=1) * sq_ref[...]
                elif kind == 1:
                    t = t * ck_ref[...] + pltpu.roll(t, HEAD_DIM, axis=1) * sk_ref[...]
                lanes = slice(slab * LANES, (slab + 1) * LANES)
                if d == 1:
                    ref[0, 0, :, lanes] = t.astype(BF16)
                else:
                    scr_ref[slot] = t
                    for r in range(d):
                        ref[0, r, :, lanes] = scr_ref[slot, pl.ds(r, tm // d, stride=d), :].astype(BF16)
                    slot += 1


def _qkv_dil(x2d, w, cq, sq, ck, sk, *, tm, batch, seq, dilations):
    n, d_model = x2d.shape
    hd = w.shape[1] // 3
    gw = hd // len(dilations)
    tiles_per_seq = seq // tm
    kernel = functools.partial(_qkv_dil_kernel, hd=hd, dilations=dilations, tm=tm)
    tab = pl.BlockSpec((tm, LANES), lambda t: (t % tiles_per_seq, 0))
    out_specs, out_shape = [], []
    for d in dilations:
        for _ in range(3):
            out_specs.append(pl.BlockSpec((1, d, tm // d, gw),
                                          lambda t: (t // tiles_per_seq, 0, t % tiles_per_seq, 0)))
            out_shape.append(jax.ShapeDtypeStruct((batch, d, seq // d, gw), BF16))
    n_slots = sum(3 * (gw // LANES) for d in dilations if d > 1)
    return pl.pallas_call(
        kernel,
        grid=(n // tm,),
        in_specs=[pl.BlockSpec((tm, d_model), lambda t: (t, 0)),
                  _resident((d_model, 3 * hd), lambda t: (0, 0)),
                  tab, tab, tab, tab],
        out_specs=out_specs,
        out_shape=out_shape,
        scratch_shapes=[pltpu.VMEM((max(n_slots, 1), tm, LANES), F32)],
        compiler_params=_compiler_params(("parallel",)),
        name="qkv_dilated",
    )(x2d, w, cq, sq, ck, sk)


def _dilated_kernel(q_ref, kc_ref, kh_ref, vc_ref, vh_ref, o_ref, lse_ref, *, tl, span):
    wb = DIL_BLOCK
    n = pl.program_id(2)
    q = q_ref[...]
    k_all = jnp.concatenate([kh_ref[...], kc_ref[...]], axis=0)
    v_all = jnp.concatenate([vh_ref[...], vc_ref[...]], axis=0)
    lane = lax.broadcasted_iota(jnp.int32, (wb, LANES), 1)
    first_head_lanes = (lane % HEAD_DIM) < HALF_DIM
    first_head_out = lane < HEAD_DIM
    qq = lax.broadcasted_iota(jnp.int32, (wb, 2 * wb), 0)
    kk = lax.broadcasted_iota(jnp.int32, (wb, 2 * wb), 1)
    dist = qq + wb - kk
    band = jnp.logical_and(dist >= 0, dist <= span)
    nt_dims = (((1,), (1,)), ((), ()))
    for pair in range(q.shape[1] // LANES):
        lo, hi = pair * LANES, (pair + 1) * LANES
        for sub in range(tl // wb):
            qs = q[sub * wb:(sub + 1) * wb, lo:hi]
            ks = k_all[sub * wb:sub * wb + 2 * wb, lo:hi]
            vs = v_all[sub * wb:sub * wb + 2 * wb, lo:hi]
            if sub == 0:
                mask = jnp.logical_and(band, jnp.logical_or(kk >= wb, n > 0))
            else:
                mask = band
            outs, lses = [], []
            for h in range(2):
                keep = first_head_lanes if h == 0 else jnp.logical_not(first_head_lanes)
                qm = jnp.where(keep, qs, jnp.zeros_like(qs))
                s = lax.dot_general(qm, ks, nt_dims, preferred_element_type=F32)
                s = jnp.where(mask, s, NEG_INF)
                m = jnp.max(s, axis=1, keepdims=True)
                p = jnp.exp2(s - m)
                l = jnp.sum(p, axis=1, keepdims=True)
                pv = jnp.dot(p.astype(BF16), vs, preferred_element_type=F32)
                outs.append(pv * (1.0 / l))
                lses.append(m + jnp.log2(l))
            o_ref[sub * wb:(sub + 1) * wb, lo:hi] = jnp.where(first_head_out, outs[0], outs[1]).astype(BF16)
            lse_ref[sub * wb:(sub + 1) * wb, lo:hi] = jnp.where(
                first_head_out, jnp.broadcast_to(lses[0], (wb, LANES)), jnp.broadcast_to(lses[1], (wb, LANES)))


def _dilated_group(q, k, v, *, group, window, dilation):
    batch, _, L, gw = q.shape
    span = window // dilation
    assert span <= DIL_BLOCK and L % DIL_BLOCK == 0
    tl = min(4 * DIL_BLOCK, L)
    halo_per_tile = tl // DIL_BLOCK
    cur = pl.BlockSpec((None, None, tl, gw), lambda b, r, n: (b, r, n, 0))
    halo = pl.BlockSpec((None, None, DIL_BLOCK, gw),
                        lambda b, r, n: (b, r, jnp.maximum(n * halo_per_tile - 1, 0), 0))
    kernel = functools.partial(_dilated_kernel, tl=tl, span=span)
    return pl.pallas_call(
        kernel,
        grid=(batch, dilation, L // tl),
        in_specs=[cur, cur, halo, cur, halo],
        out_specs=[cur, cur],
        out_shape=[jax.ShapeDtypeStruct(q.shape, BF16), jax.ShapeDtypeStruct(q.shape, F32)],
        compiler_params=_compiler_params(("parallel", "parallel", "arbitrary")),
        name=f"dilated_attention_g{group}",
    )(q, k, k, v, v)


def _layer_norm(y, g, b):
    mu = jnp.mean(y, axis=-1, keepdims=True)
    yc = y - mu
    var = jnp.mean(yc * yc, axis=-1, keepdims=True)
    return yc * lax.rsqrt(var + LN_EPS) * g + b


def _position_order(ref, d, scr_ref, slot, tm):
    if d == 1:
        return ref[0, 0]
    n_slabs = ref.shape[-1] // LANES
    for slab in range(n_slabs):
        for r in range(d):
            scr_ref[slot + slab, pl.ds(r, tm // d, stride=d), :] = (
                ref[0, r, :, slab * LANES:(slab + 1) * LANES].astype(F32))
    return jnp.concatenate([scr_ref[slot + slab] for slab in range(n_slabs)], axis=1)


def _tail_kernel(*refs, dilations, tm, ff_chunk):
    n_groups = len(dilations)
    n_mix = 2 * n_groups if n_groups > 1 else 1
    mix_refs = refs[:n_mix]
    (x_ref, wo_ref, g1_ref, b1_ref, win_ref, wout_ref, g2_ref, b2_ref, out_ref) = refs[n_mix:n_mix + 9]
    scr_ref = refs[n_mix + 9] if len(refs) > n_mix + 9 else None
    if n_groups == 1:
        o = _position_order(mix_refs[0], dilations[0], scr_ref, 0, tm)
    else:
        slabs = mix_refs[0].shape[-1] // LANES
        outs = [_position_order(mix_refs[g], d, scr_ref, 2 * g * slabs, tm).astype(F32)
                for g, d in enumerate(dilations)]
        lses = [_position_order(mix_refs[n_groups + g], d, scr_ref, (2 * g + 1) * slabs, tm)
                for g, d in enumerate(dilations)]
        top = functools.reduce(jnp.maximum, lses)
        ws = [jnp.exp2(l - top) for l in lses]
        inv = 1.0 / functools.reduce(jnp.add, ws)
        o = jnp.concatenate([(outs[g] * (ws[g] * inv)).astype(BF16) for g in range(n_groups)], axis=1)
    mix = jnp.dot(o, wo_ref[...], preferred_element_type=F32)
    h = _layer_norm(DEEPNORM_ALPHA * x_ref[...] + mix, g1_ref[...], b1_ref[...])
    hb = h.astype(BF16)
    d_ff = win_ref.shape[1]
    mlp = jnp.zeros_like(h)
    for c in range(d_ff // ff_chunk):
        lo, hi = c * ff_chunk, (c + 1) * ff_chunk
        hid = jnp.maximum(jnp.dot(hb, win_ref[:, lo:hi], preferred_element_type=F32), 0.0)
        mlp = mlp + jnp.dot((hid * hid).astype(BF16), wout_ref[lo:hi, :], preferred_element_type=F32)
    out_ref[...] = _layer_norm(DEEPNORM_ALPHA * h + mlp, g2_ref[...], b2_ref[...])


def _layer_tail(mix_inputs, dilations, x2d, wo, g1, b1, win, wout, g2, b2, *, tm, seq):
    n, d_model = x2d.shape
    d_ff = win.shape[1]
    tiles_per_seq = seq // tm
    n_groups = len(dilations)

    def mix_spec(a, d):
        return pl.BlockSpec((1, d, tm // d, a.shape[-1]),
                            lambda t: (t // tiles_per_seq, 0, t % tiles_per_seq, 0))

    vec = _resident((1, d_model), lambda t: (0, 0))
    kernel = functools.partial(_tail_kernel, dilations=dilations, tm=tm, ff_chunk=min(1024, d_ff))
    slabs = mix_inputs[0].shape[-1] // LANES
    n_slots = 2 * n_groups * slabs if any(d > 1 for d in dilations) else 0
    return pl.pallas_call(
        kernel,
        grid=(n // tm,),
        in_specs=[mix_spec(a, dilations[i % n_groups]) for i, a in enumerate(mix_inputs)] + [
            pl.BlockSpec((tm, d_model), lambda t: (t, 0)),
            _resident(wo.shape, lambda t: (0, 0)),
            vec, vec,
            _resident(win.shape, lambda t: (0, 0)),
            _resident(wout.shape, lambda t: (0, 0)),
            vec, vec,
        ],
        out_specs=pl.BlockSpec((tm, d_model), lambda t: (t, 0)),
        out_shape=jax.ShapeDtypeStruct((n, d_model), F32),
        scratch_shapes=[pltpu.VMEM((n_slots, tm, LANES), F32)] if n_slots else [],
        compiler_params=_compiler_params(("parallel",)),
        name=f"layer_tail_{n_groups}",
    )(*mix_inputs, x2d, wo, g1, b1, win, wout, g2, b2)


def kernel(x, moba_w_qkv, moba_w_o, dil_w_qkv, dil_w_o, mlp_w_in, mlp_w_out,
           ln_mix_g, ln_mix_b, ln_mlp_g, ln_mlp_b):
    B, S, D = x.shape
    n_tok = B * S
    tm = min(512, S)
    q_scale = HEAD_DIM ** -0.5 * LOG2E
    cos, sin = _rotary_tables(S)
    cq, sq = _pair_tables(cos, sin, q_scale)
    ck, sk = _pair_tables(cos, sin, 1.0)
    vec = lambda a: a.reshape(1, D).astype(F32)

    w = moba_w_qkv[0]
    hd = w.shape[1] // 3
    perm = _rope_pair_perm(hd // HEAD_DIM)
    wq = w[:, :hd][:, perm]
    wk = w[:, hd:2 * hd][:, perm]
    wv = w[:, 2 * hd:]
    k, qt, vt, kmean = _qkv_moba(x, wk.astype(BF16), wq.T.astype(BF16), wv.T.astype(BF16),
                                 ck, sk, cq.T, sq.T, tm=tm)
    attn = _moba_attention(qt, k, vt, kmean.reshape(B, S // MOBA_BLOCK, hd), unroll=4, pairs=1)
    h = _layer_tail([attn.reshape(B, 1, S, hd)], (1,), x.reshape(n_tok, D), moba_w_o[0].astype(BF16),
                    vec(ln_mix_g[0]), vec(ln_mix_b[0]), mlp_w_in[0].astype(BF16), mlp_w_out[0].astype(BF16),
                    vec(ln_mlp_g[0]), vec(ln_mlp_b[0]), tm=tm, seq=S)

    w = dil_w_qkv[0]
    hd = w.shape[1] // 3
    perm = _rope_pair_perm(hd // HEAD_DIM)
    w_all = jnp.concatenate([w[:, :hd][:, perm], w[:, hd:2 * hd][:, perm], w[:, 2 * hd:]], axis=1)
    dilations = tuple(d for _, d in DIL_GROUPS)
    qkv = _qkv_dil(h, w_all.astype(BF16), cq, sq, ck, sk, tm=tm, batch=B, seq=S, dilations=dilations)
    outs, lses = [], []
    for g, (window, dilation) in enumerate(DIL_GROUPS):
        o_g, lse_g = _dilated_group(*qkv[3 * g:3 * g + 3], group=g, window=window, dilation=dilation)
        outs.append(o_g)
        lses.append(lse_g)
    h = _layer_tail(outs + lses, dilations, h, dil_w_o[0].astype(BF16),
                    vec(ln_mix_g[1]), vec(ln_mix_b[1]), mlp_w_in[1].astype(BF16), mlp_w_out[1].astype(BF16),
                    vec(ln_mlp_g[1]), vec(ln_mlp_b[1]), tm=tm, seq=S)
    return h.reshape(B, S, D)
```

```python
import functools
import math

import jax
import jax.numpy as jnp
from jax import lax
from jax.experimental import pallas as pl
from jax.experimental.pallas import tpu as pltpu

HEAD_DIM = 64
HALF_DIM = HEAD_DIM // 2
ROPE_THETA = 10000.0
LN_EPS = 1e-5
DEPTH = 2
DEEPNORM_ALPHA = (2.0 * DEPTH) ** 0.25
MOBA_BLOCK = 256
MOBA_TOPK = 3
DIL_GROUPS = ((128, 1), (512, 4), (2048, 16))
DIL_HEADS_PER_GROUP = 4
DIL_BLOCK = 128

LANES = 128
BF16_SUBLANES = 16
VMEM_LIMIT_BYTES = 56 * 1024 * 1024

LOG2E = math.log2(math.e)

F32 = jnp.float32
BF16 = jnp.bfloat16
NEG_INF = float("-inf")
POS_INF = float("inf")


def _compiler_params(semantics):
    return pltpu.CompilerParams(dimension_semantics=semantics,
                                vmem_limit_bytes=VMEM_LIMIT_BYTES)


def _resident(block_shape, index_map):
    return pl.BlockSpec(block_shape, index_map, pipeline_mode=pl.Buffered(1))


def _rope_pair_perm(n_heads):
    idx = []
    for p in range(n_heads // 2):
        a, b = 2 * p * HEAD_DIM, (2 * p + 1) * HEAD_DIM
        for base in (a, b, a + HALF_DIM, b + HALF_DIM):
            idx.extend(range(base, base + HALF_DIM))
    return jnp.asarray(idx, dtype=jnp.int32)


def _rotary_tables(seq):
    inv = 1.0 / (ROPE_THETA ** (jnp.arange(0, HEAD_DIM, 2, dtype=F32) / HEAD_DIM))
    ang = jnp.arange(seq, dtype=F32)[:, None] * inv[None, :]
    return jnp.cos(ang), jnp.sin(ang)


def _pair_tables(cos, sin, scale):
    c = jnp.concatenate([cos, cos, cos, cos], axis=-1) * scale
    s = jnp.concatenate([-sin, -sin, sin, sin], axis=-1) * scale
    return c, s


def _qkv_moba_kernel(x_ref, wk_ref, wqt_ref, wvt_ref, ck_ref, sk_ref, cqt_ref, sqt_ref,
                     k_ref, qt_ref, vt_ref, kmean_ref, *, tm, n_pairs):
    blocks_per_tile = tm // MOBA_BLOCK
    xb = x_ref[0].astype(BF16)

    k = jnp.dot(xb, wk_ref[...], preferred_element_type=F32)
    ck = ck_ref[...]
    sk = sk_ref[...]
    for g in range(n_pairs):
        t = k[:, g * LANES:(g + 1) * LANES]
        kr = t * ck + pltpu.roll(t, HEAD_DIM, axis=1) * sk
        k_ref[0, :, g * LANES:(g + 1) * LANES] = kr.astype(BF16)
        for blk in range(blocks_per_tile):
            mean = jnp.mean(kr[blk * MOBA_BLOCK:(blk + 1) * MOBA_BLOCK], axis=0, keepdims=True)
            kmean_ref[0, 0, blk:blk + 1, g * LANES:(g + 1) * LANES] = mean

    nt_dims = (((1,), (1,)), ((), ()))
    qt = lax.dot_general(wqt_ref[...], xb, nt_dims, preferred_element_type=F32)
    cq = cqt_ref[...]
    sq = sqt_ref[...]
    for g in range(n_pairs):
        t = qt[g * LANES:(g + 1) * LANES]
        swapped = jnp.concatenate([t[HEAD_DIM:], t[:HEAD_DIM]], axis=0)
        qr = (t * cq + swapped * sq).astype(BF16)
        for j in range(blocks_per_tile):
            qt_ref[0, g, j] = qr[:, j * MOBA_BLOCK:(j + 1) * MOBA_BLOCK]

    vt = lax.dot_general(wvt_ref[...], xb, nt_dims, preferred_element_type=F32).astype(BF16)
    for g in range(n_pairs):
        for j in range(blocks_per_tile):
            vt_ref[0, g, j] = vt[g * LANES:(g + 1) * LANES, j * MOBA_BLOCK:(j + 1) * MOBA_BLOCK]


def _qkv_moba(x, wk, wqt, wvt, ck, sk, cqt, sqt, *, tm):
    B, S, D = x.shape
    hd = wk.shape[1]
    n_pairs = hd // LANES
    nb = S // MOBA_BLOCK
    kernel = functools.partial(_qkv_moba_kernel, tm=tm, n_pairs=n_pairs)
    bpt = tm // MOBA_BLOCK
    return pl.pallas_call(
        kernel,
        grid=(B, S // tm),
        in_specs=[
            pl.BlockSpec((1, tm, D), lambda b, t: (b, t, 0)),
            _resident((D, hd), lambda b, t: (0, 0)),
            _resident((hd, D), lambda b, t: (0, 0)),
            _resident((hd, D), lambda b, t: (0, 0)),
            pl.BlockSpec((tm, LANES), lambda b, t: (t, 0)),
            pl.BlockSpec((tm, LANES), lambda b, t: (t, 0)),
            pl.BlockSpec((LANES, tm), lambda b, t: (0, t)),
            pl.BlockSpec((LANES, tm), lambda b, t: (0, t)),
        ],
        out_specs=[
            pl.BlockSpec((1, tm, hd), lambda b, t: (b, t, 0)),
            pl.BlockSpec((1, n_pairs, bpt, LANES, MOBA_BLOCK), lambda b, t: (b, 0, t, 0, 0)),
            pl.BlockSpec((1, n_pairs, bpt, LANES, MOBA_BLOCK), lambda b, t: (b, 0, t, 0, 0)),
            pl.BlockSpec((1, 1, bpt, hd), lambda b, t: (b, t, 0, 0)),
        ],
        out_shape=[
            jax.ShapeDtypeStruct((B, S, hd), BF16),
            jax.ShapeDtypeStruct((B, n_pairs, nb, LANES, MOBA_BLOCK), BF16),
            jax.ShapeDtypeStruct((B, n_pairs, nb, LANES, MOBA_BLOCK), BF16),
            jax.ShapeDtypeStruct((B, S // tm, bpt, hd), F32),
        ],
        compiler_params=_compiler_params(("parallel", "parallel")),
        name="qkv_moba",
    )(x, wk, wqt, wvt, ck, sk, cqt, sqt)


def _top3_blocks(gate, n_past):
    nb = gate.shape[0]
    row_i = lax.broadcasted_iota(jnp.int32, gate.shape, 0)
    row = row_i.astype(F32)
    g = jnp.where(row_i < n_past, gate, NEG_INF)
    sel = jnp.zeros(gate.shape, F32)
    for _ in range(MOBA_TOPK):
        mx = jnp.max(g, axis=0, keepdims=True)
        is_mx = jnp.logical_and(g == mx, g > NEG_INF)
        first = jnp.min(jnp.where(is_mx, row, nb), axis=0, keepdims=True)
        pick = row == first
        sel = jnp.where(pick, 1.0, sel)
        g = jnp.where(pick, NEG_INF, g)
    return sel


def _moba_kernel(qt_ref, k_ref, vt_ref, kmean_ref, o_ref, sel_ref, acc_ref, *, nb, unroll, pairs, lookahead):
    blk = MOBA_BLOCK
    n_heads = 2 * pairs
    n_items = nb + 1
    kmean = kmean_ref[0].astype(BF16)
    zeros_half = jnp.zeros((HALF_DIM, blk), BF16)
    ones_rows = jnp.ones((BF16_SUBLANES, blk), BF16)
    key_idx = lax.broadcasted_iota(jnp.int32, (blk, blk), 0)
    qry_idx = lax.broadcasted_iota(jnp.int32, (blk, blk), 1)
    causal = key_idx <= qry_idx

    def q_head(tile, h):
        qt = qt_ref[0, h // 2, tile]
        lo = (h % 2) * HALF_DIM
        rows = [zeros_half] * (4 * pairs)
        rows[4 * (h // 2) + h % 2] = qt[lo:lo + HALF_DIM]
        rows[4 * (h // 2) + 2 + h % 2] = qt[HEAD_DIM + lo:HEAD_DIM + lo + HALF_DIM]
        return jnp.concatenate(rows, axis=0)

    def values_aug(j, h):
        vt_pair = vt_ref[0, h // 2, j]
        return jnp.concatenate([vt_pair[(h % 2) * HEAD_DIM:(h % 2 + 1) * HEAD_DIM], ones_rows], axis=0)

    acc_ref[...] = jnp.zeros(acc_ref.shape, F32)

    def tile_pair(a, carry):
        tiles = (a, nb - 1 - a)

        def item(t):
            if t < 2:
                return t == 0, t, tiles[t], tiles[t], True
            past = t - 2
            first = past < a
            return first, jnp.where(first, 0, 1), jnp.where(first, a, nb - 1 - a), jnp.where(first, past, past - a), False

        def chain_scores(t, h):
            _, _, tile, block, _ = item(t)
            k_j = k_ref[0, pl.ds(pl.multiple_of(block * blk, blk), blk), :]
            return jnp.dot(k_j, q_head(tile, h), preferred_element_type=F32)

        for x in range(2):
            for h in range(n_heads):
                gate = jnp.dot(kmean, q_head(tiles[x], h), preferred_element_type=F32)
                sel_ref[x, h] = _top3_blocks(gate, tiles[x])

        neg = jnp.full((1, blk), NEG_INF, F32)
        m_run = {(x, h, u): neg for x in range(2) for h in range(n_heads) for u in range(unroll)}

        n_chains = n_items * n_heads
        scores = {j: chain_scores(j // n_heads, j % n_heads) for j in range(lookahead)}
        for j in range(n_chains):
            t, h = j // n_heads, j % n_heads
            u = t % unroll
            first, x, _, block, own = item(t)
            s = scores.pop(j)
            if j + lookahead < n_chains:
                scores[j + lookahead] = chain_scores((j + lookahead) // n_heads, (j + lookahead) % n_heads)
            if own:
                s = jnp.where(causal, s, NEG_INF)
                m_old = m_run[(x, h, u)]
                m_new = jnp.maximum(m_old, jnp.max(s, axis=0, keepdims=True))
                m_run[(x, h, u)] = m_new
                shift = m_new
            else:
                chosen = sel_ref[x, h, pl.ds(block, 1), :] > 0.0
                m_old = jnp.where(first, m_run[(0, h, u)], m_run[(1, h, u)])
                m_new = jnp.maximum(m_old, jnp.where(chosen, jnp.max(s, axis=0, keepdims=True), NEG_INF))
                m_run[(0, h, u)] = jnp.where(first, m_new, m_run[(0, h, u)])
                m_run[(1, h, u)] = jnp.where(first, m_run[(1, h, u)], m_new)
                m_new = jnp.where(m_new == NEG_INF, 0.0, m_new)
                shift = jnp.where(chosen, m_new, POS_INF)
            p = jnp.exp2(s - shift).astype(BF16)
            pv = jnp.dot(values_aug(block, h), p, preferred_element_type=F32)
            acc_ref[x, h, u] = acc_ref[x, h, u] * jnp.exp2(m_old - m_new) + pv

        for x in range(2):
            outs = []
            for h in range(n_heads):
                ms = [m_run[(x, h, u)] for u in range(unroll)]
                m_all = functools.reduce(jnp.maximum, ms)
                acc = functools.reduce(
                    jnp.add, [acc_ref[x, h, u] * jnp.exp2(ms[u] - m_all) for u in range(unroll)])
                outs.append(acc[:HEAD_DIM] * (1.0 / acc[HEAD_DIM:HEAD_DIM + 1]))
            o_all = jnp.concatenate(outs, axis=0)
            o_ref[0, pl.ds(pl.multiple_of(tiles[x] * blk, blk), blk), :] = o_all.T.astype(BF16)
        return carry

    lax.fori_loop(0, nb // 2, tile_pair, 0)


def _moba_attention(qt, k, vt, kmean, *, unroll, pairs, lookahead):
    B, n_pairs, nb, _, blk = qt.shape
    S = k.shape[1]
    hd = k.shape[2]
    assert nb % 2 == 0 and n_pairs % pairs == 0
    n_heads = 2 * pairs
    width = pairs * LANES
    kernel = functools.partial(_moba_kernel, nb=nb, unroll=unroll, pairs=pairs, lookahead=lookahead)
    return pl.pallas_call(
        kernel,
        grid=(B, n_pairs // pairs),
        in_specs=[
            pl.BlockSpec((1, pairs, nb, LANES, blk), lambda b, p: (b, p, 0, 0, 0)),
            pl.BlockSpec((1, S, width), lambda b, p: (b, 0, p)),
            pl.BlockSpec((1, pairs, nb, LANES, blk), lambda b, p: (b, p, 0, 0, 0)),
            pl.BlockSpec((1, nb, width), lambda b, p: (b, 0, p)),
        ],
        out_specs=pl.BlockSpec((1, S, width), lambda b, p: (b, 0, p)),
        out_shape=jax.ShapeDtypeStruct((B, S, hd), BF16),
        scratch_shapes=[
            pltpu.VMEM((2, n_heads, nb, blk), F32),
            pltpu.VMEM((2, n_heads, unroll, HEAD_DIM + BF16_SUBLANES, blk), F32),
        ],
        compiler_params=_compiler_params(("parallel", "parallel")),
        name="moba_attention",
    )(qt, k, vt, kmean)


def _qkv_dil_kernel(x_ref, w_ref, cq_ref, sq_ref, ck_ref, sk_ref, *refs, hd, dilations, tm):
    out_refs, scr_ref = refs[:-1], refs[-1]
    gw = hd // len(dilations)
    xb = x_ref[...].astype(BF16)
    qkv = jnp.dot(xb, w_ref[...], preferred_element_type=F32)
    slot = 0
    for g, d in enumerate(dilations):
        for kind in range(3):
            ref = out_refs[3 * g + kind]
            for slab in range(gw // LANES):
                col = kind * hd + g * gw + slab * LANES
                t = qkv[:, col:col + LANES]
                if kind == 0:
                    t = t * cq_ref[...] + pltpu.roll(t, HEAD_DIM, axis=1) * sq_ref[...]
                elif kind == 1:
                    t = t * ck_ref[...] + pltpu.roll(t, HEAD_DIM, axis=1) * sk_ref[...]
                lanes = slice(slab * LANES, (slab + 1) * LANES)
                if d == 1:
                    ref[0, 0, :, lanes] = t.astype(BF16)
                else:
                    scr_ref[slot] = t
                    for r in range(d):
                        ref[0, r, :, lanes] = scr_ref[slot, pl.ds(r, tm // d, stride=d), :].astype(BF16)
                    slot += 1


def _qkv_dil(x2d, w, cq, sq, ck, sk, *, tm, batch, seq, dilations):
    n, d_model = x2d.shape
    hd = w.shape[1] // 3
    gw = hd // len(dilations)
    tiles_per_seq = seq // tm
    kernel = functools.partial(_qkv_dil_kernel, hd=hd, dilations=dilations, tm=tm)
    tab = pl.BlockSpec((tm, LANES), lambda t: (t % tiles_per_seq, 0))
    out_specs, out_shape = [], []
    for d in dilations:
        for _ in range(3):
            out_specs.append(pl.BlockSpec((1, d, tm // d, gw),
                                          lambda t: (t // tiles_per_seq, 0, t % tiles_per_seq, 0)))
            out_shape.append(jax.ShapeDtypeStruct((batch, d, seq // d, gw), BF16))
    n_slots = sum(3 * (gw // LANES) for d in dilations if d > 1)
    return pl.pallas_call(
        kernel,
        grid=(n // tm,),
        in_specs=[pl.BlockSpec((tm, d_model), lambda t: (t, 0)),
                  _resident((d_model, 3 * hd), lambda t: (0, 0)),
                  tab, tab, tab, tab],
        out_specs=out_specs,
        out_shape=out_shape,
        scratch_shapes=[pltpu.VMEM((max(n_slots, 1), tm, LANES), F32)],
        compiler_params=_compiler_params(("parallel",)),
        name="qkv_dilated",
    )(x2d, w, cq, sq, ck, sk)


def _dilated_kernel(q_ref, kc_ref, kh_ref, vc_ref, vh_ref, o_ref, lse_ref, *, tl, span):
    wb = DIL_BLOCK
    n = pl.program_id(2)
    q = q_ref[...]
    k_all = jnp.concatenate([kh_ref[...], kc_ref[...]], axis=0)
    v_all = jnp.concatenate([vh_ref[...], vc_ref[...]], axis=0)
    lane = lax.broadcasted_iota(jnp.int32, (wb, LANES), 1)
    first_head_lanes = (lane % HEAD_DIM) < HALF_DIM
    first_head_out = lane < HEAD_DIM
    qq = lax.broadcasted_iota(jnp.int32, (wb, 2 * wb), 0)
    kk = lax.broadcasted_iota(jnp.int32, (wb, 2 * wb), 1)
    dist = qq + wb - kk
    band = jnp.logical_and(dist >= 0, dist <= span)
    nt_dims = (((1,), (1,)), ((), ()))
    for pair in range(q.shape[1] // LANES):
        lo, hi = pair * LANES, (pair + 1) * LANES
        for sub in range(tl // wb):
            qs = q[sub * wb:(sub + 1) * wb, lo:hi]
            ks = k_all[sub * wb:sub * wb + 2 * wb, lo:hi]
            vs = v_all[sub * wb:sub * wb + 2 * wb, lo:hi]
            if sub == 0:
                mask = jnp.logical_and(band, jnp.logical_or(kk >= wb, n > 0))
            else:
                mask = band
            outs, lses = [], []
            for h in range(2):
                keep = first_head_lanes if h == 0 else jnp.logical_not(first_head_lanes)
                qm = jnp.where(keep, qs, jnp.zeros_like(qs))
                s = lax.dot_general(qm, ks, nt_dims, preferred_element_type=F32)
                s = jnp.where(mask, s, NEG_INF)
                m = jnp.max(s, axis=1, keepdims=True)
                p = jnp.exp2(s - m)
                l = jnp.sum(p, axis=1, keepdims=True)
                pv = jnp.dot(p.astype(BF16), vs, preferred_element_type=F32)
                outs.append(pv * (1.0 / l))
                lses.append(m + jnp.log2(l))
            o_ref[sub * wb:(sub + 1) * wb, lo:hi] = jnp.where(first_head_out, outs[0], outs[1]).astype(BF16)
            lse_ref[sub * wb:(sub + 1) * wb, lo:hi] = jnp.where(
                first_head_out, jnp.broadcast_to(lses[0], (wb, LANES)), jnp.broadcast_to(lses[1], (wb, LANES)))


def _dilated_group(q, k, v, *, group, window, dilation):
    batch, _, L, gw = q.shape
    span = window // dilation
    assert span <= DIL_BLOCK and L % DIL_BLOCK == 0
    tl = min(4 * DIL_BLOCK, L)
    halo_per_tile = tl // DIL_BLOCK
    cur = pl.BlockSpec((None, None, tl, gw), lambda b, r, n: (b, r, n, 0))
    halo = pl.BlockSpec((None, None, DIL_BLOCK, gw),
                        lambda b, r, n: (b, r, jnp.maximum(n * halo_per_tile - 1, 0), 0))
    kernel = functools.partial(_dilated_kernel, tl=tl, span=span)
    return pl.pallas_call(
        kernel,
        grid=(batch, dilation, L // tl),
        in_specs=[cur, cur, halo, cur, halo],
        out_specs=[cur, cur],
        out_shape=[jax.ShapeDtypeStruct(q.shape, BF16), jax.ShapeDtypeStruct(q.shape, F32)],
        compiler_params=_compiler_params(("parallel", "parallel", "arbitrary")),
        name=f"dilated_attention_g{group}",
    )(q, k, k, v, v)


def _layer_norm(y, g, b):
    mu = jnp.mean(y, axis=-1, keepdims=True)
    yc = y - mu
    var = jnp.mean(yc * yc, axis=-1, keepdims=True)
    return yc * lax.rsqrt(var + LN_EPS) * g + b


def _position_order(ref, d, scr_ref, slot, tm):
    if d == 1:
        return ref[0, 0]
    n_slabs = ref.shape[-1] // LANES
    for slab in range(n_slabs):
        for r in range(d):
            scr_ref[slot + slab, pl.ds(r, tm // d, stride=d), :] = (
                ref[0, r, :, slab * LANES:(slab + 1) * LANES].astype(F32))
    return jnp.concatenate([scr_ref[slot + slab] for slab in range(n_slabs)], axis=1)


def _tail_kernel(*refs, dilations, tm, ff_chunk):
    n_groups = len(dilations)
    n_mix = 2 * n_groups if n_groups > 1 else 1
    mix_refs = refs[:n_mix]
    (x_ref, wo_ref, g1_ref, b1_ref, win_ref, wout_ref, g2_ref, b2_ref, out_ref) = refs[n_mix:n_mix + 9]
    scr_ref = refs[n_mix + 9] if len(refs) > n_mix + 9 else None
    if n_groups == 1:
        o = _position_order(mix_refs[0], dilations[0], scr_ref, 0, tm)
    else:
        slabs = mix_refs[0].shape[-1] // LANES
        outs = [_position_order(mix_refs[g], d, scr_ref, 2 * g * slabs, tm).astype(F32)
                for g, d in enumerate(dilations)]
        lses = [_position_order(mix_refs[n_groups + g], d, scr_ref, (2 * g + 1) * slabs, tm)
                for g, d in enumerate(dilations)]
        top = functools.reduce(jnp.maximum, lses)
        ws = [jnp.exp2(l - top) for l in lses]
        inv = 1.0 / functools.reduce(jnp.add, ws)
        o = jnp.concatenate([(outs[g] * (ws[g] * inv)).astype(BF16) for g in range(n_groups)], axis=1)
    mix = jnp.dot(o, wo_ref[...], preferred_element_type=F32)
    h = _layer_norm(DEEPNORM_ALPHA * x_ref[...] + mix, g1_ref[...], b1_ref[...])
    hb = h.astype(BF16)
    d_ff = win_ref.shape[1]
    mlp = jnp.zeros_like(h)
    for c in range(d_ff // ff_chunk):
        lo, hi = c * ff_chunk, (c + 1) * ff_chunk
        hid = jnp.maximum(jnp.dot(hb, win_ref[:, lo:hi], preferred_element_type=F32), 0.0)
        mlp = mlp + jnp.dot((hid * hid).astype(BF16), wout_ref[lo:hi, :], preferred_element_type=F32)
    out_ref[...] = _layer_norm(DEEPNORM_ALPHA * h + mlp, g2_ref[...], b2_ref[...])


def _layer_tail(mix_inputs, dilations, x2d, wo, g1, b1, win, wout, g2, b2, *, tm, seq):
    n, d_model = x2d.shape
    d_ff = win.shape[1]
    tiles_per_seq = seq // tm
    n_groups = len(dilations)

    def mix_spec(a, d):
        return pl.BlockSpec((1, d, tm // d, a.shape[-1]),
                            lambda t: (t // tiles_per_seq, 0, t % tiles_per_seq, 0))

    vec = _resident((1, d_model), lambda t: (0, 0))
    kernel = functools.partial(_tail_kernel, dilations=dilations, tm=tm, ff_chunk=min(1024, d_ff))
    slabs = mix_inputs[0].shape[-1] // LANES
    n_slots = 2 * n_groups * slabs if any(d > 1 for d in dilations) else 0
    return pl.pallas_call(
        kernel,
        grid=(n // tm,),
        in_specs=[mix_spec(a, dilations[i % n_groups]) for i, a in enumerate(mix_inputs)] + [
            pl.BlockSpec((tm, d_model), lambda t: (t, 0)),
            _resident(wo.shape, lambda t: (0, 0)),
            vec, vec,
            _resident(win.shape, lambda t: (0, 0)),
            _resident(wout.shape, lambda t: (0, 0)),
            vec, vec,
        ],
        out_specs=pl.BlockSpec((tm, d_model), lambda t: (t, 0)),
        out_shape=jax.ShapeDtypeStruct((n, d_model), F32),
        scratch_shapes=[pltpu.VMEM((n_slots, tm, LANES), F32)] if n_slots else [],
        compiler_params=_compiler_params(("parallel",)),
        name=f"layer_tail_{n_groups}",
    )(*mix_inputs, x2d, wo, g1, b1, win, wout, g2, b2)


def kernel(x, moba_w_qkv, moba_w_o, dil_w_qkv, dil_w_o, mlp_w_in, mlp_w_out,
           ln_mix_g, ln_mix_b, ln_mlp_g, ln_mlp_b):
    B, S, D = x.shape
    n_tok = B * S
    tm = min(512, S)
    q_scale = HEAD_DIM ** -0.5 * LOG2E
    cos, sin = _rotary_tables(S)
    cq, sq = _pair_tables(cos, sin, q_scale)
    ck, sk = _pair_tables(cos, sin, 1.0)
    vec = lambda a: a.reshape(1, D).astype(F32)

    w = moba_w_qkv[0]
    hd = w.shape[1] // 3
    perm = _rope_pair_perm(hd // HEAD_DIM)
    wq = w[:, :hd][:, perm]
    wk = w[:, hd:2 * hd][:, perm]
    wv = w[:, 2 * hd:]
    k, qt, vt, kmean = _qkv_moba(x, wk.astype(BF16), wq.T.astype(BF16), wv.T.astype(BF16),
                                 ck, sk, cq.T, sq.T, tm=tm)
    attn = _moba_attention(qt, k, vt, kmean.reshape(B, S // MOBA_BLOCK, hd), unroll=4, pairs=1, lookahead=6)
    h = _layer_tail([attn.reshape(B, 1, S, hd)], (1,), x.reshape(n_tok, D), moba_w_o[0].astype(BF16),
                    vec(ln_mix_g[0]), vec(ln_mix_b[0]), mlp_w_in[0].astype(BF16), mlp_w_out[0].astype(BF16),
                    vec(ln_mlp_g[0]), vec(ln_mlp_b[0]), tm=tm, seq=S)

    w = dil_w_qkv[0]
    hd = w.shape[1] // 3
    perm = _rope_pair_perm(hd // HEAD_DIM)
    w_all = jnp.concatenate([w[:, :hd][:, perm], w[:, hd:2 * hd][:, perm], w[:, 2 * hd:]], axis=1)
    dilations = tuple(d for _, d in DIL_GROUPS)
    qkv = _qkv_dil(h, w_all.astype(BF16), cq, sq, ck, sk, tm=tm, batch=B, seq=S, dilations=dilations)
    outs, lses = [], []
    for g, (window, dilation) in enumerate(DIL_GROUPS):
        o_g, lse_g = _dilated_group(*qkv[3 * g:3 * g + 3], group=g, window=window, dilation=dilation)
        outs.append(o_g)
        lses.append(lse_g)
    h = _layer_tail(outs + lses, dilations, h, dil_w_o[0].astype(BF16),
                    vec(ln_mix_g[1]), vec(ln_mix_b[1]), mlp_w_in[1].astype(BF16), mlp_w_out[1].astype(BF16),
                    vec(ln_mlp_g[1]), vec(ln_mlp_b[1]), tm=tm, seq=S)
    return h.reshape(B, S, D)
```

```python
import functools
import math

import jax
import jax.numpy as jnp
from jax import lax
from jax.experimental import pallas as pl
from jax.experimental.pallas import tpu as pltpu

HEAD_DIM = 64
HALF_DIM = HEAD_DIM // 2
ROPE_THETA = 10000.0
LN_EPS = 1e-5
DEPTH = 2
DEEPNORM_ALPHA = (2.0 * DEPTH) ** 0.25
MOBA_BLOCK = 256
MOBA_TOPK = 3
DIL_GROUPS = ((128, 1), (512, 4), (2048, 16))
DIL_HEADS_PER_GROUP = 4
DIL_BLOCK = 128

LANES = 128
BF16_SUBLANES = 16
VMEM_LIMIT_BYTES = 56 * 1024 * 1024
TAIL_SUBTILES = 2

LOG2E = math.log2(math.e)

F32 = jnp.float32
BF16 = jnp.bfloat16
NEG_INF = float("-inf")
POS_INF = float("inf")


def _compiler_params(semantics):
    return pltpu.CompilerParams(dimension_semantics=semantics,
                                vmem_limit_bytes=VMEM_LIMIT_BYTES)


def _resident(block_shape, index_map):
    return pl.BlockSpec(block_shape, index_map, pipeline_mode=pl.Buffered(1))


def _rope_pair_perm(n_heads):
    idx = []
    for p in range(n_heads // 2):
        a, b = 2 * p * HEAD_DIM, (2 * p + 1) * HEAD_DIM
        for base in (a, b, a + HALF_DIM, b + HALF_DIM):
            idx.extend(range(base, base + HALF_DIM))
    return jnp.asarray(idx, dtype=jnp.int32)


def _rotary_tables(seq):
    inv = 1.0 / (ROPE_THETA ** (jnp.arange(0, HEAD_DIM, 2, dtype=F32) / HEAD_DIM))
    ang = jnp.arange(seq, dtype=F32)[:, None] * inv[None, :]
    return jnp.cos(ang), jnp.sin(ang)


def _pair_tables(cos, sin, scale):
    c = jnp.concatenate([cos, cos, cos, cos], axis=-1) * scale
    s = jnp.concatenate([-sin, -sin, sin, sin], axis=-1) * scale
    return c, s


def _qkv_moba_kernel(x_ref, wk_ref, wqt_ref, wvt_ref, ck_ref, sk_ref, cqt_ref, sqt_ref,
                     k_ref, qt_ref, vt_ref, kmean_ref, *, tm, n_pairs):
    blocks_per_tile = tm // MOBA_BLOCK
    xb = x_ref[0].astype(BF16)

    k = jnp.dot(xb, wk_ref[...], preferred_element_type=F32)
    ck = ck_ref[...]
    sk = sk_ref[...]
    for g in range(n_pairs):
        t = k[:, g * LANES:(g + 1) * LANES]
        kr = t * ck + pltpu.roll(t, HEAD_DIM, axis=1) * sk
        k_ref[0, :, g * LANES:(g + 1) * LANES] = kr.astype(BF16)
        for blk in range(blocks_per_tile):
            mean = jnp.mean(kr[blk * MOBA_BLOCK:(blk + 1) * MOBA_BLOCK], axis=0, keepdims=True)
            kmean_ref[0, 0, blk:blk + 1, g * LANES:(g + 1) * LANES] = mean

    nt_dims = (((1,), (1,)), ((), ()))
    qt = lax.dot_general(wqt_ref[...], xb, nt_dims, preferred_element_type=F32)
    cq = cqt_ref[...]
    sq = sqt_ref[...]
    for g in range(n_pairs):
        t = qt[g * LANES:(g + 1) * LANES]
        swapped = jnp.concatenate([t[HEAD_DIM:], t[:HEAD_DIM]], axis=0)
        qr = (t * cq + swapped * sq).astype(BF16)
        for j in range(blocks_per_tile):
            qt_ref[0, g, j] = qr[:, j * MOBA_BLOCK:(j + 1) * MOBA_BLOCK]

    vt = lax.dot_general(wvt_ref[...], xb, nt_dims, preferred_element_type=F32).astype(BF16)
    for g in range(n_pairs):
        for j in range(blocks_per_tile):
            vt_ref[0, g, j] = vt[g * LANES:(g + 1) * LANES, j * MOBA_BLOCK:(j + 1) * MOBA_BLOCK]


def _qkv_moba(x, wk, wqt, wvt, ck, sk, cqt, sqt, *, tm):
    B, S, D = x.shape
    hd = wk.shape[1]
    n_pairs = hd // LANES
    nb = S // MOBA_BLOCK
    kernel = functools.partial(_qkv_moba_kernel, tm=tm, n_pairs=n_pairs)
    bpt = tm // MOBA_BLOCK
    return pl.pallas_call(
        kernel,
        grid=(B, S // tm),
        in_specs=[
            pl.BlockSpec((1, tm, D), lambda b, t: (b, t, 0)),
            _resident((D, hd), lambda b, t: (0, 0)),
            _resident((hd, D), lambda b, t: (0, 0)),
            _resident((hd, D), lambda b, t: (0, 0)),
            pl.BlockSpec((tm, LANES), lambda b, t: (t, 0)),
            pl.BlockSpec((tm, LANES), lambda b, t: (t, 0)),
            pl.BlockSpec((LANES, tm), lambda b, t: (0, t)),
            pl.BlockSpec((LANES, tm), lambda b, t: (0, t)),
        ],
        out_specs=[
            pl.BlockSpec((1, tm, hd), lambda b, t: (b, t, 0)),
            pl.BlockSpec((1, n_pairs, bpt, LANES, MOBA_BLOCK), lambda b, t: (b, 0, t, 0, 0)),
            pl.BlockSpec((1, n_pairs, bpt, LANES, MOBA_BLOCK), lambda b, t: (b, 0, t, 0, 0)),
            pl.BlockSpec((1, 1, bpt, hd), lambda b, t: (b, t, 0, 0)),
        ],
        out_shape=[
            jax.ShapeDtypeStruct((B, S, hd), BF16),
            jax.ShapeDtypeStruct((B, n_pairs, nb, LANES, MOBA_BLOCK), BF16),
            jax.ShapeDtypeStruct((B, n_pairs, nb, LANES, MOBA_BLOCK), BF16),
            jax.ShapeDtypeStruct((B, S // tm, bpt, hd), F32),
        ],
        compiler_params=_compiler_params(("parallel", "parallel")),
        name="qkv_moba",
    )(x, wk, wqt, wvt, ck, sk, cqt, sqt)


def _top3_blocks(gate, n_past):
    nb = gate.shape[0]
    row_i = lax.broadcasted_iota(jnp.int32, gate.shape, 0)
    row = row_i.astype(F32)
    g = jnp.where(row_i < n_past, gate, NEG_INF)
    sel = jnp.zeros(gate.shape, F32)
    for _ in range(MOBA_TOPK):
        mx = jnp.max(g, axis=0, keepdims=True)
        is_mx = jnp.logical_and(g == mx, g > NEG_INF)
        first = jnp.min(jnp.where(is_mx, row, nb), axis=0, keepdims=True)
        pick = row == first
        sel = jnp.where(pick, 1.0, sel)
        g = jnp.where(pick, NEG_INF, g)
    return sel


def _moba_kernel(qt_ref, k_ref, vt_ref, kmean_ref, o_ref, sel_ref, acc_ref, *, nb, unroll, pairs, lookahead):
    blk = MOBA_BLOCK
    n_heads = 2 * pairs
    n_items = nb + 1
    kmean = kmean_ref[0].astype(BF16)
    zeros_half = jnp.zeros((HALF_DIM, blk), BF16)
    ones_rows = jnp.ones((BF16_SUBLANES, blk), BF16)
    key_idx = lax.broadcasted_iota(jnp.int32, (blk, blk), 0)
    qry_idx = lax.broadcasted_iota(jnp.int32, (blk, blk), 1)
    causal = key_idx <= qry_idx

    def q_head(tile, h):
        qt = qt_ref[0, h // 2, tile]
        lo = (h % 2) * HALF_DIM
        rows = [zeros_half] * (4 * pairs)
        rows[4 * (h // 2) + h % 2] = qt[lo:lo + HALF_DIM]
        rows[4 * (h // 2) + 2 + h % 2] = qt[HEAD_DIM + lo:HEAD_DIM + lo + HALF_DIM]
        return jnp.concatenate(rows, axis=0)

    def values_aug(j, h):
        vt_pair = vt_ref[0, h // 2, j]
        return jnp.concatenate([vt_pair[(h % 2) * HEAD_DIM:(h % 2 + 1) * HEAD_DIM], ones_rows], axis=0)

    acc_ref[...] = jnp.zeros(acc_ref.shape, F32)

    def tile_pair(a, carry):
        tiles = (a, nb - 1 - a)

        def item(t):
            if t < 2:
                return t == 0, t, tiles[t], tiles[t], True
            past = t - 2
            first = past < a
            return first, jnp.where(first, 0, 1), jnp.where(first, a, nb - 1 - a), jnp.where(first, past, past - a), False

        def chain_scores(t, h):
            _, _, tile, block, _ = item(t)
            k_j = k_ref[0, pl.ds(pl.multiple_of(block * blk, blk), blk), :]
            return jnp.dot(k_j, q_head(tile, h), preferred_element_type=F32)

        for x in range(2):
            for h in range(n_heads):
                gate = jnp.dot(kmean, q_head(tiles[x], h), preferred_element_type=F32)
                sel_ref[x, h] = _top3_blocks(gate, tiles[x])

        neg = jnp.full((1, blk), NEG_INF, F32)
        m_run = {(x, h, u): neg for x in range(2) for h in range(n_heads) for u in range(unroll)}

        n_chains = n_items * n_heads
        scores = {j: chain_scores(j // n_heads, j % n_heads) for j in range(lookahead)}
        for j in range(n_chains):
            t, h = j // n_heads, j % n_heads
            u = t % unroll
            first, x, _, block, own = item(t)
            s = scores.pop(j)
            if j + lookahead < n_chains:
                scores[j + lookahead] = chain_scores((j + lookahead) // n_heads, (j + lookahead) % n_heads)
            if own:
                s = jnp.where(causal, s, NEG_INF)
                m_old = m_run[(x, h, u)]
                m_new = jnp.maximum(m_old, jnp.max(s, axis=0, keepdims=True))
                m_run[(x, h, u)] = m_new
                shift = m_new
            else:
                chosen = sel_ref[x, h, pl.ds(block, 1), :] > 0.0
                m_old = jnp.where(first, m_run[(0, h, u)], m_run[(1, h, u)])
                m_new = jnp.maximum(m_old, jnp.where(chosen, jnp.max(s, axis=0, keepdims=True), NEG_INF))
                m_run[(0, h, u)] = jnp.where(first, m_new, m_run[(0, h, u)])
                m_run[(1, h, u)] = jnp.where(first, m_run[(1, h, u)], m_new)
                m_new = jnp.where(m_new == NEG_INF, 0.0, m_new)
                shift = jnp.where(chosen, m_new, POS_INF)
            p = jnp.exp2(s - shift).astype(BF16)
            pv = jnp.dot(values_aug(block, h), p, preferred_element_type=F32)
            acc_ref[x, h, u] = acc_ref[x, h, u] * jnp.exp2(m_old - m_new) + pv

        for x in range(2):
            outs = []
            for h in range(n_heads):
                ms = [m_run[(x, h, u)] for u in range(unroll)]
                m_all = functools.reduce(jnp.maximum, ms)
                acc = functools.reduce(
                    jnp.add, [acc_ref[x, h, u] * jnp.exp2(ms[u] - m_all) for u in range(unroll)])
                outs.append(acc[:HEAD_DIM] * (1.0 / acc[HEAD_DIM:HEAD_DIM + 1]))
            o_all = jnp.concatenate(outs, axis=0)
            o_ref[0, pl.ds(pl.multiple_of(tiles[x] * blk, blk), blk), :] = o_all.T.astype(BF16)
        return carry

    lax.fori_loop(0, nb // 2, tile_pair, 0)


def _moba_attention(qt, k, vt, kmean, *, unroll, pairs, lookahead):
    B, n_pairs, nb, _, blk = qt.shape
    S = k.shape[1]
    hd = k.shape[2]
    assert nb % 2 == 0 and n_pairs % pairs == 0
    n_heads = 2 * pairs
    width = pairs * LANES
    kernel = functools.partial(_moba_kernel, nb=nb, unroll=unroll, pairs=pairs, lookahead=lookahead)
    return pl.pallas_call(
        kernel,
        grid=(B, n_pairs // pairs),
        in_specs=[
            pl.BlockSpec((1, pairs, nb, LANES, blk), lambda b, p: (b, p, 0, 0, 0)),
            pl.BlockSpec((1, S, width), lambda b, p: (b, 0, p)),
            pl.BlockSpec((1, pairs, nb, LANES, blk), lambda b, p: (b, p, 0, 0, 0)),
            pl.BlockSpec((1, nb, width), lambda b, p: (b, 0, p)),
        ],
        out_specs=pl.BlockSpec((1, S, width), lambda b, p: (b, 0, p)),
        out_shape=jax.ShapeDtypeStruct((B, S, hd), BF16),
        scratch_shapes=[
            pltpu.VMEM((2, n_heads, nb, blk), F32),
            pltpu.VMEM((2, n_heads, unroll, HEAD_DIM + BF16_SUBLANES, blk), F32),
        ],
        compiler_params=_compiler_params(("parallel", "parallel")),
        name="moba_attention",
    )(qt, k, vt, kmean)


def _qkv_dil_kernel(x_ref, w_ref, cq_ref, sq_ref, ck_ref, sk_ref, *refs, hd, dilations, tm):
    out_refs, scr_ref = refs[:-1], refs[-1]
    gw = hd // len(dilations)
    xb = x_ref[...].astype(BF16)
    qkv = jnp.dot(xb, w_ref[...], preferred_element_type=F32)
    slot = 0
    for g, d in enumerate(dilations):
        for kind in range(3):
            ref = out_refs[3 * g + kind]
            for slab in range(gw // LANES):
                col = kind * hd + g * gw + slab * LANES
                t = qkv[:, col:col + LANES]
                if kind == 0:
                    t = t * cq_ref[...] + pltpu.roll(t, HEAD_DIM, axis=1) * sq_ref[...]
                elif kind == 1:
                    t = t * ck_ref[...] + pltpu.roll(t, HEAD_DIM, axis=1) * sk_ref[...]
                lanes = slice(slab * LANES, (slab + 1) * LANES)
                if d == 1:
                    ref[0, 0, :, lanes] = t.astype(BF16)
                else:
                    scr_ref[slot] = t
                    for r in range(d):
                        ref[0, r, :, lanes] = scr_ref[slot, pl.ds(r, tm // d, stride=d), :].astype(BF16)
                    slot += 1


def _qkv_dil(x2d, w, cq, sq, ck, sk, *, tm, batch, seq, dilations):
    n, d_model = x2d.shape
    hd = w.shape[1] // 3
    gw = hd // len(dilations)
    tiles_per_seq = seq // tm
    kernel = functools.partial(_qkv_dil_kernel, hd=hd, dilations=dilations, tm=tm)
    tab = pl.BlockSpec((tm, LANES), lambda t: (t % tiles_per_seq, 0))
    out_specs, out_shape = [], []
    for d in dilations:
        for _ in range(3):
            out_specs.append(pl.BlockSpec((1, d, tm // d, gw),
                                          lambda t: (t // tiles_per_seq, 0, t % tiles_per_seq, 0)))
            out_shape.append(jax.ShapeDtypeStruct((batch, d, seq // d, gw), BF16))
    n_slots = sum(3 * (gw // LANES) for d in dilations if d > 1)
    return pl.pallas_call(
        kernel,
        grid=(n // tm,),
        in_specs=[pl.BlockSpec((tm, d_model), lambda t: (t, 0)),
                  _resident((d_model, 3 * hd), lambda t: (0, 0)),
                  tab, tab, tab, tab],
        out_specs=out_specs,
        out_shape=out_shape,
        scratch_shapes=[pltpu.VMEM((max(n_slots, 1), tm, LANES), F32)],
        compiler_params=_compiler_params(("parallel",)),
        name="qkv_dilated",
    )(x2d, w, cq, sq, ck, sk)


def _dilated_kernel(q_ref, kc_ref, kh_ref, vc_ref, vh_ref, o_ref, lse_ref, *, tl, span):
    wb = DIL_BLOCK
    n = pl.program_id(2)
    q = q_ref[...]
    k_all = jnp.concatenate([kh_ref[...], kc_ref[...]], axis=0)
    v_all = jnp.concatenate([vh_ref[...], vc_ref[...]], axis=0)
    lane = lax.broadcasted_iota(jnp.int32, (wb, LANES), 1)
    first_head_lanes = (lane % HEAD_DIM) < HALF_DIM
    first_head_out = lane < HEAD_DIM
    qq = lax.broadcasted_iota(jnp.int32, (wb, 2 * wb), 0)
    kk = lax.broadcasted_iota(jnp.int32, (wb, 2 * wb), 1)
    dist = qq + wb - kk
    band = jnp.logical_and(dist >= 0, dist <= span)
    nt_dims = (((1,), (1,)), ((), ()))
    for pair in range(q.shape[1] // LANES):
        lo, hi = pair * LANES, (pair + 1) * LANES
        for sub in range(tl // wb):
            qs = q[sub * wb:(sub + 1) * wb, lo:hi]
            ks = k_all[sub * wb:sub * wb + 2 * wb, lo:hi]
            vs = v_all[sub * wb:sub * wb + 2 * wb, lo:hi]
            if sub == 0:
                mask = jnp.logical_and(band, jnp.logical_or(kk >= wb, n > 0))
            else:
                mask = band
            outs, lses = [], []
            for h in range(2):
                keep = first_head_lanes if h == 0 else jnp.logical_not(first_head_lanes)
                qm = jnp.where(keep, qs, jnp.zeros_like(qs))
                s = lax.dot_general(qm, ks, nt_dims, preferred_element_type=F32)
                s = jnp.where(mask, s, NEG_INF)
                m = jnp.max(s, axis=1, keepdims=True)
                p = jnp.exp2(s - m)
                l = jnp.sum(p, axis=1, keepdims=True)
                pv = jnp.dot(p.astype(BF16), vs, preferred_element_type=F32)
                outs.append(pv * (1.0 / l))
                lses.append(m + jnp.log2(l))
            o_ref[sub * wb:(sub + 1) * wb, lo:hi] = jnp.where(first_head_out, outs[0], outs[1]).astype(BF16)
            lse_ref[sub * wb:(sub + 1) * wb, lo:hi] = jnp.where(
                first_head_out, jnp.broadcast_to(lses[0], (wb, LANES)), jnp.broadcast_to(lses[1], (wb, LANES)))


def _dilated_group(q, k, v, *, group, window, dilation):
    batch, _, L, gw = q.shape
    span = window // dilation
    assert span <= DIL_BLOCK and L % DIL_BLOCK == 0
    tl = min(8 * DIL_BLOCK, L)
    halo_per_tile = tl // DIL_BLOCK
    cur = pl.BlockSpec((None, None, tl, gw), lambda b, r, n: (b, r, n, 0))
    halo = pl.BlockSpec((None, None, DIL_BLOCK, gw),
                        lambda b, r, n: (b, r, jnp.maximum(n * halo_per_tile - 1, 0), 0))
    kernel = functools.partial(_dilated_kernel, tl=tl, span=span)
    return pl.pallas_call(
        kernel,
        grid=(batch, dilation, L // tl),
        in_specs=[cur, cur, halo, cur, halo],
        out_specs=[cur, cur],
        out_shape=[jax.ShapeDtypeStruct(q.shape, BF16), jax.ShapeDtypeStruct(q.shape, F32)],
        compiler_params=_compiler_params(("parallel", "parallel", "arbitrary")),
        name=f"dilated_attention_g{group}",
    )(q, k, k, v, v)


def _layer_norm(y, g, b):
    mu = jnp.mean(y, axis=-1, keepdims=True)
    yc = y - mu
    var = jnp.mean(yc * yc, axis=-1, keepdims=True)
    return yc * lax.rsqrt(var + LN_EPS) * g + b


def _position_order(ref, d, scr_ref, slot, tm):
    if d == 1:
        return ref[0, 0]
    n_slabs = ref.shape[-1] // LANES
    for slab in range(n_slabs):
        for r in range(d):
            scr_ref[slot + slab, pl.ds(r, tm // d, stride=d), :] = (
                ref[0, r, :, slab * LANES:(slab + 1) * LANES].astype(F32))
    return jnp.concatenate([scr_ref[slot + slab] for slab in range(n_slabs)], axis=1)


def _tail_kernel(*refs, dilations, tm, ff_chunk):
    n_groups = len(dilations)
    n_mix = 2 * n_groups if n_groups > 1 else 1
    mix_refs = refs[:n_mix]
    (x_ref, wo_ref, g1_ref, b1_ref, win_ref, wout_ref, g2_ref, b2_ref, out_ref) = refs[n_mix:n_mix + 9]
    scr_ref = refs[n_mix + 9] if len(refs) > n_mix + 9 else None
    if n_groups == 1:
        o = _position_order(mix_refs[0], dilations[0], scr_ref, 0, tm)
    else:
        slabs = mix_refs[0].shape[-1] // LANES
        outs = [_position_order(mix_refs[g], d, scr_ref, 2 * g * slabs, tm).astype(F32)
                for g, d in enumerate(dilations)]
        lses = [_position_order(mix_refs[n_groups + g], d, scr_ref, (2 * g + 1) * slabs, tm)
                for g, d in enumerate(dilations)]
        top = functools.reduce(jnp.maximum, lses)
        ws = [jnp.exp2(l - top) for l in lses]
        inv = 1.0 / functools.reduce(jnp.add, ws)
        o = jnp.concatenate([(outs[g] * (ws[g] * inv)).astype(BF16) for g in range(n_groups)], axis=1)
    d_ff = win_ref.shape[1]
    rows = tm // TAIL_SUBTILES

    def mlp(hb):
        acc = jnp.zeros(hb.shape, F32)
        for c in range(d_ff // ff_chunk):
            lo, hi = c * ff_chunk, (c + 1) * ff_chunk
            hid = jnp.maximum(jnp.dot(hb, win_ref[:, lo:hi], preferred_element_type=F32), 0.0)
            acc = acc + jnp.dot((hid * hid).astype(BF16), wout_ref[lo:hi, :], preferred_element_type=F32)
        return acc

    sub = [slice(r * rows, (r + 1) * rows) for r in range(TAIL_SUBTILES)]
    mixes = [jnp.dot(o[rs], wo_ref[...], preferred_element_type=F32) for rs in sub]
    hs, mlps = [], []
    for r, rs in enumerate(sub):
        hs.append(_layer_norm(DEEPNORM_ALPHA * x_ref[rs, :] + mixes[r], g1_ref[...], b1_ref[...]))
        mlps.append(mlp(hs[r].astype(BF16)))
    for r, rs in enumerate(sub):
        out_ref[rs, :] = _layer_norm(DEEPNORM_ALPHA * hs[r] + mlps[r], g2_ref[...], b2_ref[...])


def _layer_tail(mix_inputs, dilations, x2d, wo, g1, b1, win, wout, g2, b2, *, tm, seq):
    n, d_model = x2d.shape
    d_ff = win.shape[1]
    tiles_per_seq = seq // tm
    n_groups = len(dilations)

    def mix_spec(a, d):
        return pl.BlockSpec((1, d, tm // d, a.shape[-1]),
                            lambda t: (t // tiles_per_seq, 0, t % tiles_per_seq, 0))

    vec = _resident((1, d_model), lambda t: (0, 0))
    kernel = functools.partial(_tail_kernel, dilations=dilations, tm=tm, ff_chunk=min(1024, d_ff))
    slabs = mix_inputs[0].shape[-1] // LANES
    n_slots = 2 * n_groups * slabs if any(d > 1 for d in dilations) else 0
    return pl.pallas_call(
        kernel,
        grid=(n // tm,),
        in_specs=[mix_spec(a, dilations[i % n_groups]) for i, a in enumerate(mix_inputs)] + [
            pl.BlockSpec((tm, d_model), lambda t: (t, 0)),
            _resident(wo.shape, lambda t: (0, 0)),
            vec, vec,
            _resident(win.shape, lambda t: (0, 0)),
            _resident(wout.shape, lambda t: (0, 0)),
            vec, vec,
        ],
        out_specs=pl.BlockSpec((tm, d_model), lambda t: (t, 0)),
        out_shape=jax.ShapeDtypeStruct((n, d_model), F32),
        scratch_shapes=[pltpu.VMEM((n_slots, tm, LANES), F32)] if n_slots else [],
        compiler_params=_compiler_params(("parallel",)),
        name=f"layer_tail_{n_groups}",
    )(*mix_inputs, x2d, wo, g1, b1, win, wout, g2, b2)


def kernel(x, moba_w_qkv, moba_w_o, dil_w_qkv, dil_w_o, mlp_w_in, mlp_w_out,
           ln_mix_g, ln_mix_b, ln_mlp_g, ln_mlp_b):
    B, S, D = x.shape
    n_tok = B * S
    tm = min(512, S)
    q_scale = HEAD_DIM ** -0.5 * LOG2E
    cos, sin = _rotary_tables(S)
    cq, sq = _pair_tables(cos, sin, q_scale)
    ck, sk = _pair_tables(cos, sin, 1.0)
    vec = lambda a: a.reshape(1, D).astype(F32)

    w = moba_w_qkv[0]
    hd = w.shape[1] // 3
    perm = _rope_pair_perm(hd // HEAD_DIM)
    wq = w[:, :hd][:, perm]
    wk = w[:, hd:2 * hd][:, perm]
    wv = w[:, 2 * hd:]
    k, qt, vt, kmean = _qkv_moba(x, wk.astype(BF16), wq.T.astype(BF16), wv.T.astype(BF16),
                                 ck, sk, cq.T, sq.T, tm=tm)
    attn = _moba_attention(qt, k, vt, kmean.reshape(B, S // MOBA_BLOCK, hd), unroll=4, pairs=1, lookahead=5)
    h = _layer_tail([attn.reshape(B, 1, S, hd)], (1,), x.reshape(n_tok, D), moba_w_o[0].astype(BF16),
                    vec(ln_mix_g[0]), vec(ln_mix_b[0]), mlp_w_in[0].astype(BF16), mlp_w_out[0].astype(BF16),
                    vec(ln_mlp_g[0]), vec(ln_mlp_b[0]), tm=tm, seq=S)

    w = dil_w_qkv[0]
    hd = w.shape[1] // 3
    perm = _rope_pair_perm(hd // HEAD_DIM)
    w_all = jnp.concatenate([w[:, :hd][:, perm], w[:, hd:2 * hd][:, perm], w[:, 2 * hd:]], axis=1)
    dilations = tuple(d for _, d in DIL_GROUPS)
    qkv = _qkv_dil(h, w_all.astype(BF16), cq, sq, ck, sk, tm=tm, batch=B, seq=S, dilations=dilations)
    outs, lses = [], []
    for g, (window, dilation) in enumerate(DIL_GROUPS):
        o_g, lse_g = _dilated_group(*qkv[3 * g:3 * g + 3], group=g, window=window, dilation=dilation)
        outs.append(o_g)
        lses.append(lse_g)
    h = _layer_tail(outs + lses, dilations, h, dil_w_o[0].astype(BF16),
                    vec(ln_mix_g[1]), vec(ln_mix_b[1]), mlp_w_in[1].astype(BF16), mlp_w_out[1].astype(BF16),
                    vec(ln_mlp_g[1]), vec(ln_mlp_b[1]), tm=tm, seq=S)
    return h.reshape(B, S, D)
```

```python
import functools
import math

import jax
import jax.numpy as jnp
import numpy as np
from jax import lax
from jax.experimental import pallas as pl
from jax.experimental.pallas import tpu as pltpu

HEAD_DIM = 64
HALF_DIM = HEAD_DIM // 2
ROPE_THETA = 10000.0
LN_EPS = 1e-5
DEPTH = 2
DEEPNORM_ALPHA = (2.0 * DEPTH) ** 0.25
MOBA_BLOCK = 256
MOBA_TOPK = 3
DIL_GROUPS = ((128, 1), (512, 4), (2048, 16))
DIL_HEADS_PER_GROUP = 4
DIL_BLOCK = 128

LANES = 128
BF16_SUBLANES = 16
VMEM_LIMIT_BYTES = 56 * 1024 * 1024
TAIL_SUBTILES = 2

LOG2E = math.log2(math.e)

F32 = jnp.float32
BF16 = jnp.bfloat16
NEG_INF = float("-inf")
POS_INF = float("inf")


def _compiler_params(semantics):
    return pltpu.CompilerParams(dimension_semantics=semantics,
                                vmem_limit_bytes=VMEM_LIMIT_BYTES)


def _resident(block_shape, index_map):
    return pl.BlockSpec(block_shape, index_map, pipeline_mode=pl.Buffered(1))


def _rope_pair_perm(n_heads):
    idx = []
    for p in range(n_heads // 2):
        a, b = 2 * p * HEAD_DIM, (2 * p + 1) * HEAD_DIM
        for base in (a, b, a + HALF_DIM, b + HALF_DIM):
            idx.extend(range(base, base + HALF_DIM))
    return jnp.asarray(idx, dtype=jnp.int32)


def _rotary_tables(seq):
    inv = (1.0 / (np.float32(ROPE_THETA) ** (np.arange(0, HEAD_DIM, 2, dtype=np.float32) / np.float32(HEAD_DIM))))
    ang = np.arange(seq, dtype=np.float32)[:, None] * inv.astype(np.float32)[None, :]
    return np.cos(ang).astype(np.float32), np.sin(ang).astype(np.float32)


def _pair_tables(cos, sin, scale):
    c = np.concatenate([cos, cos, cos, cos], axis=-1) * np.float32(scale)
    s = np.concatenate([-sin, -sin, sin, sin], axis=-1) * np.float32(scale)
    return c.astype(np.float32), s.astype(np.float32)


def _qkv_moba_kernel(x_ref, wk_ref, wqt_ref, wvt_ref, ck_ref, sk_ref, cqt_ref, sqt_ref,
                     k_ref, qt_ref, vt_ref, kmean_ref, *, tm, n_pairs):
    blocks_per_tile = tm // MOBA_BLOCK
    xb = x_ref[0].astype(BF16)

    k = jnp.dot(xb, wk_ref[...], preferred_element_type=F32)
    ck = ck_ref[...]
    sk = sk_ref[...]
    for g in range(n_pairs):
        t = k[:, g * LANES:(g + 1) * LANES]
        kr = t * ck + pltpu.roll(t, HEAD_DIM, axis=1) * sk
        k_ref[0, :, g * LANES:(g + 1) * LANES] = kr.astype(BF16)
        for blk in range(blocks_per_tile):
            mean = jnp.mean(kr[blk * MOBA_BLOCK:(blk + 1) * MOBA_BLOCK], axis=0, keepdims=True)
            kmean_ref[0, 0, blk:blk + 1, g * LANES:(g + 1) * LANES] = mean

    nt_dims = (((1,), (1,)), ((), ()))
    qt = lax.dot_general(wqt_ref[...], xb, nt_dims, preferred_element_type=F32)
    cq = cqt_ref[...]
    sq = sqt_ref[...]
    for g in range(n_pairs):
        t = qt[g * LANES:(g + 1) * LANES]
        swapped = jnp.concatenate([t[HEAD_DIM:], t[:HEAD_DIM]], axis=0)
        qr = (t * cq + swapped * sq).astype(BF16)
        for j in range(blocks_per_tile):
            qt_ref[0, g, j] = qr[:, j * MOBA_BLOCK:(j + 1) * MOBA_BLOCK]

    vt = lax.dot_general(wvt_ref[...], xb, nt_dims, preferred_element_type=F32).astype(BF16)
    for g in range(n_pairs):
        for j in range(blocks_per_tile):
            vt_ref[0, g, j] = vt[g * LANES:(g + 1) * LANES, j * MOBA_BLOCK:(j + 1) * MOBA_BLOCK]


def _qkv_moba(x, wk, wqt, wvt, ck, sk, cqt, sqt, *, tm):
    B, S, D = x.shape
    hd = wk.shape[1]
    n_pairs = hd // LANES
    nb = S // MOBA_BLOCK
    kernel = functools.partial(_qkv_moba_kernel, tm=tm, n_pairs=n_pairs)
    bpt = tm // MOBA_BLOCK
    return pl.pallas_call(
        kernel,
        grid=(B, S // tm),
        in_specs=[
            pl.BlockSpec((1, tm, D), lambda b, t: (b, t, 0)),
            _resident((D, hd), lambda b, t: (0, 0)),
            _resident((hd, D), lambda b, t: (0, 0)),
            _resident((hd, D), lambda b, t: (0, 0)),
            pl.BlockSpec((tm, LANES), lambda b, t: (t, 0)),
            pl.BlockSpec((tm, LANES), lambda b, t: (t, 0)),
            pl.BlockSpec((LANES, tm), lambda b, t: (0, t)),
            pl.BlockSpec((LANES, tm), lambda b, t: (0, t)),
        ],
        out_specs=[
            pl.BlockSpec((1, tm, hd), lambda b, t: (b, t, 0)),
            pl.BlockSpec((1, n_pairs, bpt, LANES, MOBA_BLOCK), lambda b, t: (b, 0, t, 0, 0)),
            pl.BlockSpec((1, n_pairs, bpt, LANES, MOBA_BLOCK), lambda b, t: (b, 0, t, 0, 0)),
            pl.BlockSpec((1, 1, bpt, hd), lambda b, t: (b, t, 0, 0)),
        ],
        out_shape=[
            jax.ShapeDtypeStruct((B, S, hd), BF16),
            jax.ShapeDtypeStruct((B, n_pairs, nb, LANES, MOBA_BLOCK), BF16),
            jax.ShapeDtypeStruct((B, n_pairs, nb, LANES, MOBA_BLOCK), BF16),
            jax.ShapeDtypeStruct((B, S // tm, bpt, hd), F32),
        ],
        compiler_params=_compiler_params(("parallel", "parallel")),
        name="qkv_moba",
    )(x, wk, wqt, wvt, ck, sk, cqt, sqt)


def _top3_blocks(gate, n_past):
    nb = gate.shape[0]
    row_i = lax.broadcasted_iota(jnp.int32, gate.shape, 0)
    row = row_i.astype(F32)
    g = jnp.where(row_i < n_past, gate, NEG_INF)
    sel = jnp.zeros(gate.shape, F32)
    for _ in range(MOBA_TOPK):
        mx = jnp.max(g, axis=0, keepdims=True)
        is_mx = jnp.logical_and(g == mx, g > NEG_INF)
        first = jnp.min(jnp.where(is_mx, row, nb), axis=0, keepdims=True)
        pick = row == first
        sel = jnp.where(pick, 1.0, sel)
        g = jnp.where(pick, NEG_INF, g)
    return sel


def _moba_kernel(qt_ref, k_ref, vt_ref, kmean_ref, o_ref, sel_ref, acc_ref, *, nb, unroll, pairs, lookahead):
    blk = MOBA_BLOCK
    n_heads = 2 * pairs
    n_items = nb + 1
    kmean = kmean_ref[0].astype(BF16)
    zeros_half = jnp.zeros((HALF_DIM, blk), BF16)
    ones_rows = jnp.ones((BF16_SUBLANES, blk), BF16)
    key_idx = lax.broadcasted_iota(jnp.int32, (blk, blk), 0)
    qry_idx = lax.broadcasted_iota(jnp.int32, (blk, blk), 1)
    causal = key_idx <= qry_idx

    def q_head(tile, h):
        qt = qt_ref[0, h // 2, tile]
        lo = (h % 2) * HALF_DIM
        rows = [zeros_half] * (4 * pairs)
        rows[4 * (h // 2) + h % 2] = qt[lo:lo + HALF_DIM]
        rows[4 * (h // 2) + 2 + h % 2] = qt[HEAD_DIM + lo:HEAD_DIM + lo + HALF_DIM]
        return jnp.concatenate(rows, axis=0)

    def values_aug(j, h):
        vt_pair = vt_ref[0, h // 2, j]
        return jnp.concatenate([vt_pair[(h % 2) * HEAD_DIM:(h % 2 + 1) * HEAD_DIM], ones_rows], axis=0)

    acc_ref[...] = jnp.zeros(acc_ref.shape, F32)

    def tile_pair(a, carry):
        tiles = (a, nb - 1 - a)

        def item(t):
            if t < 2:
                return t == 0, t, tiles[t], tiles[t], True
            past = t - 2
            first = past < a
            return first, jnp.where(first, 0, 1), jnp.where(first, a, nb - 1 - a), jnp.where(first, past, past - a), False

        def chain_scores(t, h):
            _, _, tile, block, _ = item(t)
            k_j = k_ref[0, pl.ds(pl.multiple_of(block * blk, blk), blk), :]
            return jnp.dot(k_j, q_head(tile, h), preferred_element_type=F32)

        for x in range(2):
            for h in range(n_heads):
                gate = jnp.dot(kmean, q_head(tiles[x], h), preferred_element_type=F32)
                sel_ref[x, h] = _top3_blocks(gate, tiles[x])

        neg = jnp.full((1, blk), NEG_INF, F32)
        m_run = {(x, h, u): neg for x in range(2) for h in range(n_heads) for u in range(unroll)}

        n_chains = n_items * n_heads
        scores = {j: chain_scores(j // n_heads, j % n_heads) for j in range(lookahead)}
        for j in range(n_chains):
            t, h = j // n_heads, j % n_heads
            u = t % unroll
            first, x, _, block, own = item(t)
            s = scores.pop(j)
            if j + lookahead < n_chains:
                scores[j + lookahead] = chain_scores((j + lookahead) // n_heads, (j + lookahead) % n_heads)
            if own:
                s = jnp.where(causal, s, NEG_INF)
                m_old = m_run[(x, h, u)]
                m_new = jnp.maximum(m_old, jnp.max(s, axis=0, keepdims=True))
                m_run[(x, h, u)] = m_new
                shift = m_new
            else:
                chosen = sel_ref[x, h, pl.ds(block, 1), :] > 0.0
                m_old = jnp.where(first, m_run[(0, h, u)], m_run[(1, h, u)])
                m_new = jnp.maximum(m_old, jnp.where(chosen, jnp.max(s, axis=0, keepdims=True), NEG_INF))
                m_run[(0, h, u)] = jnp.where(first, m_new, m_run[(0, h, u)])
                m_run[(1, h, u)] = jnp.where(first, m_run[(1, h, u)], m_new)
                m_new = jnp.where(m_new == NEG_INF, 0.0, m_new)
                shift = jnp.where(chosen, m_new, POS_INF)
            p = jnp.exp2(s - shift).astype(BF16)
            pv = jnp.dot(values_aug(block, h), p, preferred_element_type=F32)
            acc_ref[x, h, u] = acc_ref[x, h, u] * jnp.exp2(m_old - m_new) + pv

        for x in range(2):
            outs = []
            for h in range(n_heads):
                ms = [m_run[(x, h, u)] for u in range(unroll)]
                m_all = functools.reduce(jnp.maximum, ms)
                acc = functools.reduce(
                    jnp.add, [acc_ref[x, h, u] * jnp.exp2(ms[u] - m_all) for u in range(unroll)])
                outs.append(acc[:HEAD_DIM] * (1.0 / acc[HEAD_DIM:HEAD_DIM + 1]))
            o_all = jnp.concatenate(outs, axis=0)
            o_ref[0, pl.ds(pl.multiple_of(tiles[x] * blk, blk), blk), :] = o_all.T.astype(BF16)
        return carry

    lax.fori_loop(0, nb // 2, tile_pair, 0)


def _moba_attention(qt, k, vt, kmean, *, unroll, pairs, lookahead):
    B, n_pairs, nb, _, blk = qt.shape
    S = k.shape[1]
    hd = k.shape[2]
    assert nb % 2 == 0 and n_pairs % pairs == 0
    n_heads = 2 * pairs
    width = pairs * LANES
    kernel = functools.partial(_moba_kernel, nb=nb, unroll=unroll, pairs=pairs, lookahead=lookahead)
    return pl.pallas_call(
        kernel,
        grid=(B, n_pairs // pairs),
        in_specs=[
            pl.BlockSpec((1, pairs, nb, LANES, blk), lambda b, p: (b, p, 0, 0, 0)),
            pl.BlockSpec((1, S, width), lambda b, p: (b, 0, p)),
            pl.BlockSpec((1, pairs, nb, LANES, blk), lambda b, p: (b, p, 0, 0, 0)),
            pl.BlockSpec((1, nb, width), lambda b, p: (b, 0, p)),
        ],
        out_specs=pl.BlockSpec((1, S, width), lambda b, p: (b, 0, p)),
        out_shape=jax.ShapeDtypeStruct((B, S, hd), BF16),
        scratch_shapes=[
            pltpu.VMEM((2, n_heads, nb, blk), F32),
            pltpu.VMEM((2, n_heads, unroll, HEAD_DIM + BF16_SUBLANES, blk), F32),
        ],
        compiler_params=_compiler_params(("parallel", "parallel")),
        name="moba_attention",
    )(qt, k, vt, kmean)


def _qkv_dil_kernel(x_ref, w_ref, cq_ref, sq_ref, ck_ref, sk_ref, *refs, hd, dilations, tm):
    out_refs, scr_ref = refs[:-1], refs[-1]
    gw = hd // len(dilations)
    xb = x_ref[...].astype(BF16)
    qkv = jnp.dot(xb, w_ref[...], preferred_element_type=F32)
    slot = 0
    for g, d in enumerate(dilations):
        for kind in range(3):
            ref = out_refs[3 * g + kind]
            for slab in range(gw // LANES):
                col = kind * hd + g * gw + slab * LANES
                t = qkv[:, col:col + LANES]
                if kind == 0:
                    t = t * cq_ref[...] + pltpu.roll(t, HEAD_DIM, axis=1) * sq_ref[...]
                elif kind == 1:
                    t = t * ck_ref[...] + pltpu.roll(t, HEAD_DIM, axis=1) * sk_ref[...]
                lanes = slice(slab * LANES, (slab + 1) * LANES)
                if d == 1:
                    ref[0, 0, :, lanes] = t.astype(BF16)
                else:
                    scr_ref[slot] = t
                    for r in range(d):
                        ref[0, r, :, lanes] = scr_ref[slot, pl.ds(r, tm // d, stride=d), :].astype(BF16)
                    slot += 1


def _qkv_dil(x2d, w, cq, sq, ck, sk, *, tm, batch, seq, dilations):
    n, d_model = x2d.shape
    hd = w.shape[1] // 3
    gw = hd // len(dilations)
    tiles_per_seq = seq // tm
    kernel = functools.partial(_qkv_dil_kernel, hd=hd, dilations=dilations, tm=tm)
    tab = pl.BlockSpec((tm, LANES), lambda t: (t % tiles_per_seq, 0))
    out_specs, out_shape = [], []
    for d in dilations:
        for _ in range(3):
            out_specs.append(pl.BlockSpec((1, d, tm // d, gw),
                                          lambda t: (t // tiles_per_seq, 0, t % tiles_per_seq, 0)))
            out_shape.append(jax.ShapeDtypeStruct((batch, d, seq // d, gw), BF16))
    n_slots = sum(3 * (gw // LANES) for d in dilations if d > 1)
    return pl.pallas_call(
        kernel,
        grid=(n // tm,),
        in_specs=[pl.BlockSpec((tm, d_model), lambda t: (t, 0)),
                  _resident((d_model, 3 * hd), lambda t: (0, 0)),
                  tab, tab, tab, tab],
        out_specs=out_specs,
        out_shape=out_shape,
        scratch_shapes=[pltpu.VMEM((max(n_slots, 1), tm, LANES), F32)],
        compiler_params=_compiler_params(("parallel",)),
        name="qkv_dilated",
    )(x2d, w, cq, sq, ck, sk)


def _dilated_kernel(q_ref, kc_ref, kh_ref, vc_ref, vh_ref, o_ref, lse_ref, *, tl, span):
    wb = DIL_BLOCK
    n_blk = pl.program_id(2)
    q = q_ref[...]
    k_all = jnp.concatenate([kh_ref[...], kc_ref[...]], axis=0)
    v_all = jnp.concatenate([vh_ref[...], vc_ref[...]], axis=0)
    lane = lax.broadcasted_iota(jnp.int32, (wb, LANES), 1)
    first_head_lanes = (lane % HEAD_DIM) < HALF_DIM
    first_head_out = lane < HEAD_DIM
    qq = lax.broadcasted_iota(jnp.int32, (wb, 2 * wb), 0)
    kk = lax.broadcasted_iota(jnp.int32, (wb, 2 * wb), 1)
    dist = qq + wb - kk
    band = jnp.logical_and(dist >= 0, dist <= span)
    nt_dims = (((1,), (1,)), ((), ()))
    units = [(pair, sub, h) for pair in range(q.shape[1] // LANES) for sub in range(tl // wb) for h in range(2)]

    def unit_scores(pair, sub, h):
        lo, hi = pair * LANES, (pair + 1) * LANES
        qs = q[sub * wb:(sub + 1) * wb, lo:hi]
        ks = k_all[sub * wb:sub * wb + 2 * wb, lo:hi]
        keep = first_head_lanes if h == 0 else jnp.logical_not(first_head_lanes)
        qm = jnp.where(keep, qs, jnp.zeros_like(qs))
        return lax.dot_general(qm, ks, nt_dims, preferred_element_type=F32)

    ahead = 3
    scores = {n: unit_scores(*units[n]) for n in range(ahead)}
    outs, lses = {}, {}
    for n, (pair, sub, h) in enumerate(units):
        lo, hi = pair * LANES, (pair + 1) * LANES
        s = scores.pop(n)
        if n + ahead < len(units):
            scores[n + ahead] = unit_scores(*units[n + ahead])
        mask = jnp.logical_and(band, jnp.logical_or(kk >= wb, n_blk > 0)) if sub == 0 else band
        s = jnp.where(mask, s, NEG_INF)
        m = jnp.max(s, axis=1, keepdims=True)
        p = jnp.exp2(s - m)
        l = jnp.sum(p, axis=1, keepdims=True)
        vs = v_all[sub * wb:sub * wb + 2 * wb, lo:hi]
        pv = jnp.dot(p.astype(BF16), vs, preferred_element_type=F32)
        outs[h] = pv * (1.0 / l)
        lses[h] = m + jnp.log2(l)
        if h == 1:
            rows = slice(sub * wb, (sub + 1) * wb)
            o_ref[rows, lo:hi] = jnp.where(first_head_out, outs[0], outs[1]).astype(BF16)
            lse_ref[rows, lo:hi] = jnp.where(
                first_head_out, jnp.broadcast_to(lses[0], (wb, LANES)), jnp.broadcast_to(lses[1], (wb, LANES)))


def _dilated_group(q, k, v, *, group, window, dilation):
    batch, _, L, gw = q.shape
    span = window // dilation
    assert span <= DIL_BLOCK and L % DIL_BLOCK == 0
    tl = min(8 * DIL_BLOCK, L)
    halo_per_tile = tl // DIL_BLOCK
    cur = pl.BlockSpec((None, None, tl, gw), lambda b, r, n: (b, r, n, 0))
    halo = pl.BlockSpec((None, None, DIL_BLOCK, gw),
                        lambda b, r, n: (b, r, jnp.maximum(n * halo_per_tile - 1, 0), 0))
    kernel = functools.partial(_dilated_kernel, tl=tl, span=span)
    return pl.pallas_call(
        kernel,
        grid=(batch, dilation, L // tl),
        in_specs=[cur, cur, halo, cur, halo],
        out_specs=[cur, cur],
        out_shape=[jax.ShapeDtypeStruct(q.shape, BF16), jax.ShapeDtypeStruct(q.shape, F32)],
        compiler_params=_compiler_params(("parallel", "parallel", "arbitrary")),
        name=f"dilated_attention_g{group}",
    )(q, k, k, v, v)


def _layer_norm(y, g, b):
    mu = jnp.mean(y, axis=-1, keepdims=True)
    yc = y - mu
    var = jnp.mean(yc * yc, axis=-1, keepdims=True)
    return yc * lax.rsqrt(var + LN_EPS) * g + b


def _position_order(ref, d, scr_ref, slot, tm):
    if d == 1:
        return ref[0, 0]
    n_slabs = ref.shape[-1] // LANES
    for slab in range(n_slabs):
        for r in range(d):
            scr_ref[slot + slab, pl.ds(r, tm // d, stride=d), :] = (
                ref[0, r, :, slab * LANES:(slab + 1) * LANES].astype(F32))
    return jnp.concatenate([scr_ref[slot + slab] for slab in range(n_slabs)], axis=1)


def _tail_kernel(*refs, dilations, tm, ff_chunk):
    n_groups = len(dilations)
    n_mix = 2 * n_groups if n_groups > 1 else 1
    mix_refs = refs[:n_mix]
    (x_ref, wo_ref, g1_ref, b1_ref, win_ref, wout_ref, g2_ref, b2_ref, out_ref) = refs[n_mix:n_mix + 9]
    scr_ref = refs[n_mix + 9] if len(refs) > n_mix + 9 else None
    if n_groups == 1:
        o = _position_order(mix_refs[0], dilations[0], scr_ref, 0, tm)
    else:
        slabs = mix_refs[0].shape[-1] // LANES
        outs = [_position_order(mix_refs[g], d, scr_ref, 2 * g * slabs, tm).astype(F32)
                for g, d in enumerate(dilations)]
        lses = [_position_order(mix_refs[n_groups + g], d, scr_ref, (2 * g + 1) * slabs, tm)
                for g, d in enumerate(dilations)]
        top = functools.reduce(jnp.maximum, lses)
        ws = [jnp.exp2(l - top) for l in lses]
        inv = 1.0 / functools.reduce(jnp.add, ws)
        o = jnp.concatenate([(outs[g] * (ws[g] * inv)).astype(BF16) for g in range(n_groups)], axis=1)
    d_ff = win_ref.shape[1]
    rows = tm // TAIL_SUBTILES

    def mlp(hb):
        acc = jnp.zeros(hb.shape, F32)
        for c in range(d_ff // ff_chunk):
            lo, hi = c * ff_chunk, (c + 1) * ff_chunk
            hid = jnp.maximum(jnp.dot(hb, win_ref[:, lo:hi], preferred_element_type=F32), 0.0)
            acc = acc + jnp.dot((hid * hid).astype(BF16), wout_ref[lo:hi, :], preferred_element_type=F32)
        return acc

    sub = [slice(r * rows, (r + 1) * rows) for r in range(TAIL_SUBTILES)]
    mixes = [jnp.dot(o[rs], wo_ref[...], preferred_element_type=F32) for rs in sub]
    hs, mlps = [], []
    for r, rs in enumerate(sub):
        hs.append(_layer_norm(DEEPNORM_ALPHA * x_ref[rs, :] + mixes[r], g1_ref[...], b1_ref[...]))
        mlps.append(mlp(hs[r].astype(BF16)))
    for r, rs in enumerate(sub):
        out_ref[rs, :] = _layer_norm(DEEPNORM_ALPHA * hs[r] + mlps[r], g2_ref[...], b2_ref[...])


def _layer_tail(mix_inputs, dilations, x2d, wo, g1, b1, win, wout, g2, b2, *, tm, seq):
    n, d_model = x2d.shape
    d_ff = win.shape[1]
    tiles_per_seq = seq // tm
    n_groups = len(dilations)

    def mix_spec(a, d):
        return pl.BlockSpec((1, d, tm // d, a.shape[-1]),
                            lambda t: (t // tiles_per_seq, 0, t % tiles_per_seq, 0))

    vec = _resident((1, d_model), lambda t: (0, 0))
    kernel = functools.partial(_tail_kernel, dilations=dilations, tm=tm, ff_chunk=min(1024, d_ff))
    slabs = mix_inputs[0].shape[-1] // LANES
    n_slots = 2 * n_groups * slabs if any(d > 1 for d in dilations) else 0
    return pl.pallas_call(
        kernel,
        grid=(n // tm,),
        in_specs=[mix_spec(a, dilations[i % n_groups]) for i, a in enumerate(mix_inputs)] + [
            pl.BlockSpec((tm, d_model), lambda t: (t, 0)),
            _resident(wo.shape, lambda t: (0, 0)),
            vec, vec,
            _resident(win.shape, lambda t: (0, 0)),
            _resident(wout.shape, lambda t: (0, 0)),
            vec, vec,
        ],
        out_specs=pl.BlockSpec((tm, d_model), lambda t: (t, 0)),
        out_shape=jax.ShapeDtypeStruct((n, d_model), F32),
        scratch_shapes=[pltpu.VMEM((n_slots, tm, LANES), F32)] if n_slots else [],
        compiler_params=_compiler_params(("parallel",)),
        name=f"layer_tail_{n_groups}",
    )(*mix_inputs, x2d, wo, g1, b1, win, wout, g2, b2)


def kernel(x, moba_w_qkv, moba_w_o, dil_w_qkv, dil_w_o, mlp_w_in, mlp_w_out,
           ln_mix_g, ln_mix_b, ln_mlp_g, ln_mlp_b):
    B, S, D = x.shape
    n_tok = B * S
    tm = min(512, S)
    q_scale = HEAD_DIM ** -0.5 * LOG2E
    cos, sin = _rotary_tables(S)
    cq, sq = _pair_tables(cos, sin, q_scale)
    ck, sk = _pair_tables(cos, sin, 1.0)
    vec = lambda a: a.reshape(1, D).astype(F32)

    w = moba_w_qkv[0]
    hd = w.shape[1] // 3
    perm = _rope_pair_perm(hd // HEAD_DIM)
    wq = w[:, :hd][:, perm]
    wk = w[:, hd:2 * hd][:, perm]
    wv = w[:, 2 * hd:]
    k, qt, vt, kmean = _qkv_moba(x, wk.astype(BF16), wq.T.astype(BF16), wv.T.astype(BF16),
                                 ck, sk, np.ascontiguousarray(cq.T), np.ascontiguousarray(sq.T), tm=tm)
    attn = _moba_attention(qt, k, vt, kmean.reshape(B, S // MOBA_BLOCK, hd), unroll=4, pairs=1, lookahead=5)
    h = _layer_tail([attn.reshape(B, 1, S, hd)], (1,), x.reshape(n_tok, D), moba_w_o[0].astype(BF16),
                    vec(ln_mix_g[0]), vec(ln_mix_b[0]), mlp_w_in[0].astype(BF16), mlp_w_out[0].astype(BF16),
                    vec(ln_mlp_g[0]), vec(ln_mlp_b[0]), tm=tm, seq=S)

    w = dil_w_qkv[0]
    hd = w.shape[1] // 3
    perm = _rope_pair_perm(hd // HEAD_DIM)
    w_all = jnp.concatenate([w[:, :hd][:, perm], w[:, hd:2 * hd][:, perm], w[:, 2 * hd:]], axis=1)
    dilations = tuple(d for _, d in DIL_GROUPS)
    qkv = _qkv_dil(h, w_all.astype(BF16), cq, sq, ck, sk, tm=tm, batch=B, seq=S, dilations=dilations)
    outs, lses = [], []
    for g, (window, dilation) in enumerate(DIL_GROUPS):
        o_g, lse_g = _dilated_group(*qkv[3 * g:3 * g + 3], group=g, window=window, dilation=dilation)
        outs.append(o_g)
        lses.append(lse_g)
    h = _layer_tail(outs + lses, dilations, h, dil_w_o[0].astype(BF16),
                    vec(ln_mix_g[1]), vec(ln_mix_b[1]), mlp_w_in[1].astype(BF16), mlp_w_out[1].astype(BF16),
                    vec(ln_mlp_g[1]), vec(ln_mlp_b[1]), tm=tm, seq=S)
    return h.reshape(B, S, D)
```

```python
import functools
import math

import jax
import jax.numpy as jnp
from jax import lax
from jax.experimental import pallas as pl
from jax.experimental.pallas import tpu as pltpu

HEAD_DIM = 64
HALF_DIM = HEAD_DIM // 2
ROPE_THETA = 10000.0
LN_EPS = 1e-5
DEPTH = 2
DEEPNORM_ALPHA = (2.0 * DEPTH) ** 0.25
MOBA_BLOCK = 256
MOBA_TOPK = 3
DIL_GROUPS = ((128, 1), (512, 4), (2048, 16))
DIL_HEADS_PER_GROUP = 4
DIL_BLOCK = 128

LANES = 128
BF16_SUBLANES = 16
VMEM_LIMIT_BYTES = 56 * 1024 * 1024
TAIL_SUBTILES = 2

LOG2E = math.log2(math.e)

F32 = jnp.float32
BF16 = jnp.bfloat16
NEG_INF = float("-inf")
POS_INF = float("inf")


def _compiler_params(semantics):
    return pltpu.CompilerParams(dimension_semantics=semantics,
                                vmem_limit_bytes=VMEM_LIMIT_BYTES)


def _resident(block_shape, index_map):
    return pl.BlockSpec(block_shape, index_map, pipeline_mode=pl.Buffered(1))


def _pair_heads(w):
    d, hd = w.shape
    w = w.reshape(d, hd // LANES, 2, 2, HALF_DIM)
    return w.transpose(0, 1, 3, 2, 4).reshape(d, hd)


def _rotary_tables(seq, scale, transposed=False):
    inv = 1.0 / (ROPE_THETA ** (jnp.arange(0, HEAD_DIM, 2, dtype=F32) / HEAD_DIM))
    inv = jnp.tile(inv, LANES // HALF_DIM)
    sign = jnp.where(jnp.arange(LANES) < HEAD_DIM, -scale, scale).astype(F32)
    pos = jnp.arange(seq, dtype=F32)
    if transposed:
        ang = inv[:, None] * pos[None, :]
        return jnp.cos(ang) * scale, jnp.sin(ang) * sign[:, None]
    ang = pos[:, None] * inv[None, :]
    return jnp.cos(ang) * scale, jnp.sin(ang) * sign[None, :]


def _qkv_moba_kernel(x_ref, wk_ref, wqt_ref, wvt_ref, ck_ref, sk_ref, cqt_ref, sqt_ref,
                     k_ref, qt_ref, vt_ref, kmean_ref, *, tm, n_pairs):
    blocks_per_tile = tm // MOBA_BLOCK
    xb = x_ref[0].astype(BF16)

    k = jnp.dot(xb, wk_ref[...], preferred_element_type=F32)
    ck = ck_ref[...]
    sk = sk_ref[...]
    for g in range(n_pairs):
        t = k[:, g * LANES:(g + 1) * LANES]
        kr = t * ck + pltpu.roll(t, HEAD_DIM, axis=1) * sk
        k_ref[0, :, g * LANES:(g + 1) * LANES] = kr.astype(BF16)
        for blk in range(blocks_per_tile):
            mean = jnp.mean(kr[blk * MOBA_BLOCK:(blk + 1) * MOBA_BLOCK], axis=0, keepdims=True)
            kmean_ref[0, 0, blk:blk + 1, g * LANES:(g + 1) * LANES] = mean

    nt_dims = (((1,), (1,)), ((), ()))
    qt = lax.dot_general(wqt_ref[...], xb, nt_dims, preferred_element_type=F32)
    cq = cqt_ref[...]
    sq = sqt_ref[...]
    for g in range(n_pairs):
        t = qt[g * LANES:(g + 1) * LANES]
        swapped = jnp.concatenate([t[HEAD_DIM:], t[:HEAD_DIM]], axis=0)
        qr = (t * cq + swapped * sq).astype(BF16)
        for j in range(blocks_per_tile):
            qt_ref[0, g, j] = qr[:, j * MOBA_BLOCK:(j + 1) * MOBA_BLOCK]

    vt = lax.dot_general(wvt_ref[...], xb, nt_dims, preferred_element_type=F32).astype(BF16)
    for g in range(n_pairs):
        for j in range(blocks_per_tile):
            vt_ref[0, g, j] = vt[g * LANES:(g + 1) * LANES, j * MOBA_BLOCK:(j + 1) * MOBA_BLOCK]


def _qkv_moba(x, wk, wqt, wvt, ck, sk, cqt, sqt, *, tm):
    B, S, D = x.shape
    hd = wk.shape[1]
    n_pairs = hd // LANES
    nb = S // MOBA_BLOCK
    kernel = functools.partial(_qkv_moba_kernel, tm=tm, n_pairs=n_pairs)
    bpt = tm // MOBA_BLOCK
    return pl.pallas_call(
        kernel,
        grid=(B, S // tm),
        in_specs=[
            pl.BlockSpec((1, tm, D), lambda b, t: (b, t, 0)),
            _resident((D, hd), lambda b, t: (0, 0)),
            _resident((hd, D), lambda b, t: (0, 0)),
            _resident((hd, D), lambda b, t: (0, 0)),
            pl.BlockSpec((tm, LANES), lambda b, t: (t, 0)),
            pl.BlockSpec((tm, LANES), lambda b, t: (t, 0)),
            pl.BlockSpec((LANES, tm), lambda b, t: (0, t)),
            pl.BlockSpec((LANES, tm), lambda b, t: (0, t)),
        ],
        out_specs=[
            pl.BlockSpec((1, tm, hd), lambda b, t: (b, t, 0)),
            pl.BlockSpec((1, n_pairs, bpt, LANES, MOBA_BLOCK), lambda b, t: (b, 0, t, 0, 0)),
            pl.BlockSpec((1, n_pairs, bpt, LANES, MOBA_BLOCK), lambda b, t: (b, 0, t, 0, 0)),
            pl.BlockSpec((1, 1, bpt, hd), lambda b, t: (b, t, 0, 0)),
        ],
        out_shape=[
            jax.ShapeDtypeStruct((B, S, hd), BF16),
            jax.ShapeDtypeStruct((B, n_pairs, nb, LANES, MOBA_BLOCK), BF16),
            jax.ShapeDtypeStruct((B, n_pairs, nb, LANES, MOBA_BLOCK), BF16),
            jax.ShapeDtypeStruct((B, S // tm, bpt, hd), F32),
        ],
        compiler_params=_compiler_params(("parallel", "parallel")),
        name="qkv_moba",
    )(x, wk, wqt, wvt, ck, sk, cqt, sqt)


def _top3_blocks(gate, n_past):
    nb = gate.shape[0]
    row_i = lax.broadcasted_iota(jnp.int32, gate.shape, 0)
    row = row_i.astype(F32)
    g = jnp.where(row_i < n_past, gate, NEG_INF)
    sel = jnp.zeros(gate.shape, F32)
    for _ in range(MOBA_TOPK):
        mx = jnp.max(g, axis=0, keepdims=True)
        is_mx = jnp.logical_and(g == mx, g > NEG_INF)
        first = jnp.min(jnp.where(is_mx, row, nb), axis=0, keepdims=True)
        pick = row == first
        sel = jnp.where(pick, 1.0, sel)
        g = jnp.where(pick, NEG_INF, g)
    return sel


def _moba_kernel(qt_ref, k_ref, vt_ref, kmean_ref, o_ref, sel_ref, acc_ref, merged_ref, *, nb, unroll, pairs, lookahead):
    blk = MOBA_BLOCK
    n_heads = 2 * pairs
    n_items = nb + 1
    kmean = kmean_ref[0].astype(BF16)
    zeros_half = jnp.zeros((HALF_DIM, blk), BF16)
    ones_rows = jnp.ones((BF16_SUBLANES, blk), BF16)
    key_idx = lax.broadcasted_iota(jnp.int32, (blk, blk), 0)
    qry_idx = lax.broadcasted_iota(jnp.int32, (blk, blk), 1)
    causal = key_idx <= qry_idx

    def q_head(tile, h):
        qt = qt_ref[0, h // 2, tile]
        lo = (h % 2) * HALF_DIM
        rows = [zeros_half] * (4 * pairs)
        rows[4 * (h // 2) + h % 2] = qt[lo:lo + HALF_DIM]
        rows[4 * (h // 2) + 2 + h % 2] = qt[HEAD_DIM + lo:HEAD_DIM + lo + HALF_DIM]
        return jnp.concatenate(rows, axis=0)

    def values_aug(j, h):
        vt_pair = vt_ref[0, h // 2, j]
        return jnp.concatenate([vt_pair[(h % 2) * HEAD_DIM:(h % 2 + 1) * HEAD_DIM], ones_rows], axis=0)

    def pair_tiles(pair):
        return pair, nb - 1 - pair

    def choose(pair):
        tiles = pair_tiles(pair)
        return {(x, h): _top3_blocks(jnp.dot(kmean, q_head(tiles[x], h), preferred_element_type=F32), tiles[x])
                for x in range(2) for h in range(n_heads)}

    def write_merged(pair):
        tiles = pair_tiles(pair)
        for x in range(2):
            outs = []
            for h in range(n_heads):
                acc = merged_ref[x, h]
                outs.append(acc[:HEAD_DIM] * (1.0 / acc[HEAD_DIM:HEAD_DIM + 1]))
            o_all = jnp.concatenate(outs, axis=0)
            o_ref[0, pl.ds(pl.multiple_of(tiles[x] * blk, blk), blk), :] = o_all.T.astype(BF16)

    n_pairs = nb // 2
    acc_ref[...] = jnp.ones(acc_ref.shape, F32)
    merged_ref[...] = jnp.ones(merged_ref.shape, F32)
    for key, sel in choose(0).items():
        sel_ref[(0,) + key] = sel

    def tile_pair(a, carry):
        slot = a % 2
        tiles = pair_tiles(a)

        def item(t):
            if t < 2:
                return t == 0, t, tiles[t], tiles[t], True
            past = t - 2
            first = past < a
            return first, jnp.where(first, 0, 1), jnp.where(first, a, nb - 1 - a), jnp.where(first, past, past - a), False

        def chain_scores(t, h):
            _, _, tile, block, _ = item(t)
            k_j = k_ref[0, pl.ds(pl.multiple_of(block * blk, blk), blk), :]
            return jnp.dot(k_j, q_head(tile, h), preferred_element_type=F32)

        neg = jnp.full((1, blk), NEG_INF, F32)
        m_run = {(x, h, u): neg for x in range(2) for h in range(n_heads) for u in range(unroll)}

        n_chains = n_items * n_heads
        next_sel = None
        scores = {j: chain_scores(j // n_heads, j % n_heads) for j in range(lookahead)}
        for j in range(n_chains):
            if j == n_chains // 8:
                next_sel = choose(jnp.minimum(a + 1, n_pairs - 1))
            if j == n_chains // 4:
                write_merged(jnp.maximum(a - 1, 0))
            t, h = j // n_heads, j % n_heads
            u = t % unroll
            first, x, _, block, own = item(t)
            s = scores.pop(j)
            if j + lookahead < n_chains:
                scores[j + lookahead] = chain_scores((j + lookahead) // n_heads, (j + lookahead) % n_heads)
            if own:
                s = jnp.where(causal, s, NEG_INF)
                m_old = m_run[(x, h, u)]
                m_new = jnp.maximum(m_old, jnp.max(s, axis=0, keepdims=True))
                m_run[(x, h, u)] = m_new
                shift = m_new
            else:
                chosen = sel_ref[slot, x, h, pl.ds(block, 1), :] > 0.0
                m_old = jnp.where(first, m_run[(0, h, u)], m_run[(1, h, u)])
                m_new = jnp.maximum(m_old, jnp.where(chosen, jnp.max(s, axis=0, keepdims=True), NEG_INF))
                m_run[(0, h, u)] = jnp.where(first, m_new, m_run[(0, h, u)])
                m_run[(1, h, u)] = jnp.where(first, m_run[(1, h, u)], m_new)
                m_new = jnp.where(m_new == NEG_INF, 0.0, m_new)
                shift = jnp.where(chosen, m_new, POS_INF)
            p = jnp.exp2(s - shift).astype(BF16)
            pv = jnp.dot(values_aug(block, h), p, preferred_element_type=F32)
            acc_ref[x, h, u] = acc_ref[x, h, u] * jnp.exp2(m_old - m_new) + pv

        for x in range(2):
            for h in range(n_heads):
                ms = [m_run[(x, h, u)] for u in range(unroll)]
                m_all = functools.reduce(jnp.maximum, ms)
                merged_ref[x, h] = functools.reduce(
                    jnp.add, [acc_ref[x, h, u] * jnp.exp2(ms[u] - m_all) for u in range(unroll)])
        for key, sel in next_sel.items():
            sel_ref[(1 - slot,) + key] = sel
        return carry

    lax.fori_loop(0, n_pairs, tile_pair, 0)
    write_merged(n_pairs - 1)


def _moba_attention(qt, k, vt, kmean, *, unroll, pairs, lookahead):
    B, n_pairs, nb, _, blk = qt.shape
    S = k.shape[1]
    hd = k.shape[2]
    assert nb % 2 == 0 and n_pairs % pairs == 0
    n_heads = 2 * pairs
    width = pairs * LANES
    kernel = functools.partial(_moba_kernel, nb=nb, unroll=unroll, pairs=pairs, lookahead=lookahead)
    return pl.pallas_call(
        kernel,
        grid=(B, n_pairs // pairs),
        in_specs=[
            pl.BlockSpec((1, pairs, nb, LANES, blk), lambda b, p: (b, p, 0, 0, 0)),
            pl.BlockSpec((1, S, width), lambda b, p: (b, 0, p)),
            pl.BlockSpec((1, pairs, nb, LANES, blk), lambda b, p: (b, p, 0, 0, 0)),
            pl.BlockSpec((1, nb, width), lambda b, p: (b, 0, p)),
        ],
        out_specs=pl.BlockSpec((1, S, width), lambda b, p: (b, 0, p)),
        out_shape=jax.ShapeDtypeStruct((B, S, hd), BF16),
        scratch_shapes=[
            pltpu.VMEM((2, 2, n_heads, nb, blk), F32),
            pltpu.VMEM((2, n_heads, unroll, HEAD_DIM + BF16_SUBLANES, blk), F32),
            pltpu.VMEM((2, n_heads, HEAD_DIM + BF16_SUBLANES, blk), F32),
        ],
        compiler_params=_compiler_params(("parallel", "parallel")),
        name="moba_attention",
    )(qt, k, vt, kmean)


def _qkv_dil_kernel(x_ref, w_ref, cq_ref, sq_ref, ck_ref, sk_ref, *refs, hd, dilations, tm):
    out_refs, scr_ref = refs[:-1], refs[-1]
    gw = hd // len(dilations)
    xb = x_ref[...].astype(BF16)
    qkv = jnp.dot(xb, w_ref[...], preferred_element_type=F32)
    slot = 0
    for g, d in enumerate(dilations):
        for kind in range(3):
            ref = out_refs[3 * g + kind]
            for slab in range(gw // LANES):
                col = kind * hd + g * gw + slab * LANES
                t = qkv[:, col:col + LANES]
                if kind == 0:
                    t = t * cq_ref[...] + pltpu.roll(t, HEAD_DIM, axis=1) * sq_ref[...]
                elif kind == 1:
                    t = t * ck_ref[...] + pltpu.roll(t, HEAD_DIM, axis=1) * sk_ref[...]
                lanes = slice(slab * LANES, (slab + 1) * LANES)
                if d == 1:
                    ref[0, 0, :, lanes] = t.astype(BF16)
                else:
                    scr_ref[slot] = t
                    for r in range(d):
                        ref[0, r, :, lanes] = scr_ref[slot, pl.ds(r, tm // d, stride=d), :].astype(BF16)
                    slot += 1


def _qkv_dil(x2d, w, cq, sq, ck, sk, *, tm, batch, seq, dilations):
    n, d_model = x2d.shape
    hd = w.shape[1] // 3
    gw = hd // len(dilations)
    tiles_per_seq = seq // tm
    kernel = functools.partial(_qkv_dil_kernel, hd=hd, dilations=dilations, tm=tm)
    tab = pl.BlockSpec((tm, LANES), lambda t: (t % tiles_per_seq, 0))
    out_specs, out_shape = [], []
    for d in dilations:
        for _ in range(3):
            out_specs.append(pl.BlockSpec((1, d, tm // d, gw),
                                          lambda t: (t // tiles_per_seq, 0, t % tiles_per_seq, 0)))
            out_shape.append(jax.ShapeDtypeStruct((batch, d, seq // d, gw), BF16))
    n_slots = sum(3 * (gw // LANES) for d in dilations if d > 1)
    return pl.pallas_call(
        kernel,
        grid=(n // tm,),
        in_specs=[pl.BlockSpec((tm, d_model), lambda t: (t, 0)),
                  _resident((d_model, 3 * hd), lambda t: (0, 0)),
                  tab, tab, tab, tab],
        out_specs=out_specs,
        out_shape=out_shape,
        scratch_shapes=[pltpu.VMEM((max(n_slots, 1), tm, LANES), F32)],
        compiler_params=_compiler_params(("parallel",)),
        name="qkv_dilated",
    )(x2d, w, cq, sq, ck, sk)


def _dilated_kernel(q_ref, kc_ref, kh_ref, vc_ref, vh_ref, o_ref, lse_ref, *, tl, span):
    wb = DIL_BLOCK
    n_blk = pl.program_id(2)
    q = q_ref[...]
    k_all = jnp.concatenate([kh_ref[...], kc_ref[...]], axis=0)
    v_all = jnp.concatenate([vh_ref[...], vc_ref[...]], axis=0)
    lane = lax.broadcasted_iota(jnp.int32, (wb, LANES), 1)
    first_head_lanes = (lane % HEAD_DIM) < HALF_DIM
    first_head_out = lane < HEAD_DIM
    qq = lax.broadcasted_iota(jnp.int32, (wb, 2 * wb), 0)
    kk = lax.broadcasted_iota(jnp.int32, (wb, 2 * wb), 1)
    dist = qq + wb - kk
    band = jnp.logical_and(dist >= 0, dist <= span)
    nt_dims = (((1,), (1,)), ((), ()))
    units = [(pair, sub, h) for pair in range(q.shape[1] // LANES) for sub in range(tl // wb) for h in range(2)]

    def unit_scores(pair, sub, h):
        lo, hi = pair * LANES, (pair + 1) * LANES
        qs = q[sub * wb:(sub + 1) * wb, lo:hi]
        ks = k_all[sub * wb:sub * wb + 2 * wb, lo:hi]
        keep = first_head_lanes if h == 0 else jnp.logical_not(first_head_lanes)
        qm = jnp.where(keep, qs, jnp.zeros_like(qs))
        return lax.dot_general(qm, ks, nt_dims, preferred_element_type=F32)

    ahead = 3
    scores = {n: unit_scores(*units[n]) for n in range(ahead)}
    outs, lses = {}, {}
    for n, (pair, sub, h) in enumerate(units):
        lo, hi = pair * LANES, (pair + 1) * LANES
        s = scores.pop(n)
        if n + ahead < len(units):
            scores[n + ahead] = unit_scores(*units[n + ahead])
        mask = jnp.logical_and(band, jnp.logical_or(kk >= wb, n_blk > 0)) if sub == 0 else band
        s = jnp.where(mask, s, NEG_INF)
        m = jnp.max(s, axis=1, keepdims=True)
        p = jnp.exp2(s - m)
        l = jnp.sum(p, axis=1, keepdims=True)
        vs = v_all[sub * wb:sub * wb + 2 * wb, lo:hi]
        pv = jnp.dot(p.astype(BF16), vs, preferred_element_type=F32)
        outs[h] = pv * (1.0 / l)
        lses[h] = m + jnp.log2(l)
        if h == 1:
            rows = slice(sub * wb, (sub + 1) * wb)
            o_ref[rows, lo:hi] = jnp.where(first_head_out, outs[0], outs[1]).astype(BF16)
            lse_ref[rows, lo:hi] = jnp.where(
                first_head_out, jnp.broadcast_to(lses[0], (wb, LANES)), jnp.broadcast_to(lses[1], (wb, LANES)))


def _dilated_group(q, k, v, *, group, window, dilation):
    batch, _, L, gw = q.shape
    span = window // dilation
    assert span <= DIL_BLOCK and L % DIL_BLOCK == 0
    tl = min(8 * DIL_BLOCK, L)
    halo_per_tile = tl // DIL_BLOCK
    cur = pl.BlockSpec((None, None, tl, gw), lambda b, r, n: (b, r, n, 0))
    halo = pl.BlockSpec((None, None, DIL_BLOCK, gw),
                        lambda b, r, n: (b, r, jnp.maximum(n * halo_per_tile - 1, 0), 0))
    kernel = functools.partial(_dilated_kernel, tl=tl, span=span)
    return pl.pallas_call(
        kernel,
        grid=(batch, dilation, L // tl),
        in_specs=[cur, cur, halo, cur, halo],
        out_specs=[cur, cur],
        out_shape=[jax.ShapeDtypeStruct(q.shape, BF16), jax.ShapeDtypeStruct(q.shape, F32)],
        compiler_params=_compiler_params(("parallel", "parallel", "arbitrary")),
        name=f"dilated_attention_g{group}",
    )(q, k, k, v, v)


def _layer_norm(y, g, b):
    mu = jnp.mean(y, axis=-1, keepdims=True)
    yc = y - mu
    var = jnp.mean(yc * yc, axis=-1, keepdims=True)
    return yc * lax.rsqrt(var + LN_EPS) * g + b


def _position_order(ref, d, scr_ref, slot, tm):
    if d == 1:
        return ref[0, 0]
    n_slabs = ref.shape[-1] // LANES
    for slab in range(n_slabs):
        for r in range(d):
            scr_ref[slot + slab, pl.ds(r, tm // d, stride=d), :] = (
                ref[0, r, :, slab * LANES:(slab + 1) * LANES].astype(F32))
    return jnp.concatenate([scr_ref[slot + slab] for slab in range(n_slabs)], axis=1)


def _tail_kernel(*refs, dilations, tm, ff_chunk):
    n_groups = len(dilations)
    n_mix = 2 * n_groups if n_groups > 1 else 1
    mix_refs = refs[:n_mix]
    (x_ref, wo_ref, g1_ref, b1_ref, win_ref, wout_ref, g2_ref, b2_ref, out_ref) = refs[n_mix:n_mix + 9]
    scr_ref = refs[n_mix + 9] if len(refs) > n_mix + 9 else None
    if n_groups == 1:
        o = _position_order(mix_refs[0], dilations[0], scr_ref, 0, tm)
    else:
        slabs = mix_refs[0].shape[-1] // LANES
        outs = [_position_order(mix_refs[g], d, scr_ref, 2 * g * slabs, tm).astype(F32)
                for g, d in enumerate(dilations)]
        lses = [_position_order(mix_refs[n_groups + g], d, scr_ref, (2 * g + 1) * slabs, tm)
                for g, d in enumerate(dilations)]
        top = functools.reduce(jnp.maximum, lses)
        ws = [jnp.exp2(l - top) for l in lses]
        inv = 1.0 / functools.reduce(jnp.add, ws)
        o = jnp.concatenate([(outs[g] * (ws[g] * inv)).astype(BF16) for g in range(n_groups)], axis=1)
    d_ff = win_ref.shape[1]
    rows = tm // TAIL_SUBTILES

    def mlp(hb):
        acc = jnp.zeros(hb.shape, F32)
        for c in range(d_ff // ff_chunk):
            lo, hi = c * ff_chunk, (c + 1) * ff_chunk
            hid = jnp.maximum(jnp.dot(hb, win_ref[:, lo:hi], preferred_element_type=F32), 0.0)
            acc = acc + jnp.dot((hid * hid).astype(BF16), wout_ref[lo:hi, :], preferred_element_type=F32)
        return acc

    sub = [slice(r * rows, (r + 1) * rows) for r in range(TAIL_SUBTILES)]
    mixes = [jnp.dot(o[rs], wo_ref[...], preferred_element_type=F32) for rs in sub]
    hs, mlps = [], []
    for r, rs in enumerate(sub):
        hs.append(_layer_norm(DEEPNORM_ALPHA * x_ref[rs, :] + mixes[r], g1_ref[...], b1_ref[...]))
        mlps.append(mlp(hs[r].astype(BF16)))
    for r, rs in enumerate(sub):
        out_ref[rs, :] = _layer_norm(DEEPNORM_ALPHA * hs[r] + mlps[r], g2_ref[...], b2_ref[...])


def _layer_tail(mix_inputs, dilations, x2d, wo, g1, b1, win, wout, g2, b2, *, tm, seq):
    n, d_model = x2d.shape
    d_ff = win.shape[1]
    tiles_per_seq = seq // tm
    n_groups = len(dilations)

    def mix_spec(a, d):
        return pl.BlockSpec((1, d, tm // d, a.shape[-1]),
                            lambda t: (t // tiles_per_seq, 0, t % tiles_per_seq, 0))

    vec = _resident((1, d_model), lambda t: (0, 0))
    kernel = functools.partial(_tail_kernel, dilations=dilations, tm=tm, ff_chunk=min(1024, d_ff))
    slabs = mix_inputs[0].shape[-1] // LANES
    n_slots = 2 * n_groups * slabs if any(d > 1 for d in dilations) else 0
    return pl.pallas_call(
        kernel,
        grid=(n // tm,),
        in_specs=[mix_spec(a, dilations[i % n_groups]) for i, a in enumerate(mix_inputs)] + [
            pl.BlockSpec((tm, d_model), lambda t: (t, 0)),
            _resident(wo.shape, lambda t: (0, 0)),
            vec, vec,
            _resident(win.shape, lambda t: (0, 0)),
            _resident(wout.shape, lambda t: (0, 0)),
            vec, vec,
        ],
        out_specs=pl.BlockSpec((tm, d_model), lambda t: (t, 0)),
        out_shape=jax.ShapeDtypeStruct((n, d_model), F32),
        scratch_shapes=[pltpu.VMEM((n_slots, tm, LANES), F32)] if n_slots else [],
        compiler_params=_compiler_params(("parallel",)),
        name=f"layer_tail_{n_groups}",
    )(*mix_inputs, x2d, wo, g1, b1, win, wout, g2, b2)


def kernel(x, moba_w_qkv, moba_w_o, dil_w_qkv, dil_w_o, mlp_w_in, mlp_w_out,
           ln_mix_g, ln_mix_b, ln_mlp_g, ln_mlp_b):
    B, S, D = x.shape
    n_tok = B * S
    tm = min(512, S)
    q_scale = HEAD_DIM ** -0.5 * LOG2E
    cq, sq = _rotary_tables(S, q_scale)
    ck, sk = _rotary_tables(S, 1.0)
    vec = lambda a: a.reshape(1, D).astype(F32)

    w = moba_w_qkv[0]
    hd = w.shape[1] // 3
    wq = _pair_heads(w[:, :hd])
    wk = _pair_heads(w[:, hd:2 * hd])
    wv = w[:, 2 * hd:]
    k, qt, vt, kmean = _qkv_moba(x, wk.astype(BF16), wq.T.astype(BF16), wv.T.astype(BF16),
                                 ck, sk, *_rotary_tables(S, q_scale, transposed=True), tm=tm)
    attn = _moba_attention(qt, k, vt, kmean.reshape(B, S // MOBA_BLOCK, hd), unroll=4, pairs=1, lookahead=5)
    h = _layer_tail([attn.reshape(B, 1, S, hd)], (1,), x.reshape(n_tok, D), moba_w_o[0].astype(BF16),
                    vec(ln_mix_g[0]), vec(ln_mix_b[0]), mlp_w_in[0].astype(BF16), mlp_w_out[0].astype(BF16),
                    vec(ln_mlp_g[0]), vec(ln_mlp_b[0]), tm=tm, seq=S)

    w = dil_w_qkv[0]
    hd = w.shape[1] // 3
    w_all = jnp.concatenate([_pair_heads(w[:, :hd]), _pair_heads(w[:, hd:2 * hd]), w[:, 2 * hd:]], axis=1)
    dilations = tuple(d for _, d in DIL_GROUPS)
    qkv = _qkv_dil(h, w_all.astype(BF16), cq, sq, ck, sk, tm=tm, batch=B, seq=S, dilations=dilations)
    outs, lses = [], []
    for g, (window, dilation) in enumerate(DIL_GROUPS):
        o_g, lse_g = _dilated_group(*qkv[3 * g:3 * g + 3], group=g, window=window, dilation=dilation)
        outs.append(o_g)
        lses.append(lse_g)
    h = _layer_tail(outs + lses, dilations, h, dil_w_o[0].astype(BF16),
                    vec(ln_mix_g[1]), vec(ln_mix_b[1]), mlp_w_in[1].astype(BF16), mlp_w_out[1].astype(BF16),
                    vec(ln_mlp_g[1]), vec(ln_mlp_b[1]), tm=tm, seq=S)
    return h.reshape(B, S, D)
```

```python
import functools
import math

import jax
import jax.numpy as jnp
from jax import lax
from jax.experimental import pallas as pl
from jax.experimental.pallas import tpu as pltpu

HEAD_DIM = 64
HALF_DIM = HEAD_DIM // 2
ROPE_THETA = 10000.0
ROPE_SPLIT = 256
LN_EPS = 1e-5
DEPTH = 2
DEEPNORM_ALPHA = (2.0 * DEPTH) ** 0.25
MOBA_BLOCK = 256
MOBA_TOPK = 3
DIL_GROUPS = ((128, 1), (512, 4), (2048, 16))
DIL_HEADS_PER_GROUP = 4
DIL_BLOCK = 128

LANES = 128
BF16_SUBLANES = 16
VMEM_LIMIT_BYTES = 56 * 1024 * 1024
TAIL_SUBTILES = 2

LOG2E = math.log2(math.e)

F32 = jnp.float32
BF16 = jnp.bfloat16
NEG_INF = float("-inf")
POS_INF = float("inf")


def _compiler_params(semantics):
    return pltpu.CompilerParams(dimension_semantics=semantics,
                                vmem_limit_bytes=VMEM_LIMIT_BYTES)


def _resident(block_shape, index_map):
    return pl.BlockSpec(block_shape, index_map, pipeline_mode=pl.Buffered(1))


def _pair_heads(w):
    d, hd = w.shape
    w = w.reshape(d, hd // LANES, 2, 2, HALF_DIM)
    return w.transpose(0, 1, 3, 2, 4).reshape(d, hd)


def _rotary_tables(seq, scale, transposed=False):
    inv = 1.0 / (ROPE_THETA ** (jnp.arange(0, HEAD_DIM, 2, dtype=F32) / HEAD_DIM))
    inv = jnp.tile(inv, LANES // HALF_DIM)
    sign = jnp.where(jnp.arange(LANES) < HEAD_DIM, -scale, scale).astype(F32)
    coarse = jnp.arange(0, seq, ROPE_SPLIT, dtype=F32)
    fine = jnp.arange(ROPE_SPLIT, dtype=F32)
    if transposed:
        ang_a = (inv[:, None] * coarse[None, :])[:, :, None]
        ang_b = (inv[:, None] * fine[None, :])[:, None, :]
        sign, shape = sign[:, None], (LANES, seq)
    else:
        ang_a = (coarse[:, None] * inv[None, :])[:, None, :]
        ang_b = (fine[:, None] * inv[None, :])[None, :, :]
        sign, shape = sign[None, :], (seq, LANES)
    cos = (jnp.cos(ang_a) * jnp.cos(ang_b) - jnp.sin(ang_a) * jnp.sin(ang_b)).reshape(shape)
    sin = (jnp.sin(ang_a) * jnp.cos(ang_b) + jnp.cos(ang_a) * jnp.sin(ang_b)).reshape(shape)
    return cos * scale, sin * sign


def _qkv_moba_kernel(x_ref, wk_ref, wqt_ref, wvt_ref, ck_ref, sk_ref, cqt_ref, sqt_ref,
                     k_ref, qt_ref, vt_ref, kmean_ref, *, tm, n_pairs):
    blocks_per_tile = tm // MOBA_BLOCK
    xb = x_ref[0].astype(BF16)

    k = jnp.dot(xb, wk_ref[...], preferred_element_type=F32)
    ck = ck_ref[...]
    sk = sk_ref[...]
    for g in range(n_pairs):
        t = k[:, g * LANES:(g + 1) * LANES]
        kr = t * ck + pltpu.roll(t, HEAD_DIM, axis=1) * sk
        k_ref[0, :, g * LANES:(g + 1) * LANES] = kr.astype(BF16)
        for blk in range(blocks_per_tile):
            mean = jnp.mean(kr[blk * MOBA_BLOCK:(blk + 1) * MOBA_BLOCK], axis=0, keepdims=True)
            kmean_ref[0, 0, blk:blk + 1, g * LANES:(g + 1) * LANES] = mean

    nt_dims = (((1,), (1,)), ((), ()))
    qt = lax.dot_general(wqt_ref[...], xb, nt_dims, preferred_element_type=F32)
    cq = cqt_ref[...]
    sq = sqt_ref[...]
    for g in range(n_pairs):
        t = qt[g * LANES:(g + 1) * LANES]
        swapped = jnp.concatenate([t[HEAD_DIM:], t[:HEAD_DIM]], axis=0)
        qr = (t * cq + swapped * sq).astype(BF16)
        for j in range(blocks_per_tile):
            qt_ref[0, g, j] = qr[:, j * MOBA_BLOCK:(j + 1) * MOBA_BLOCK]

    vt = lax.dot_general(wvt_ref[...], xb, nt_dims, preferred_element_type=F32).astype(BF16)
    for g in range(n_pairs):
        for j in range(blocks_per_tile):
            vt_ref[0, g, j] = vt[g * LANES:(g + 1) * LANES, j * MOBA_BLOCK:(j + 1) * MOBA_BLOCK]


def _qkv_moba(x, wk, wqt, wvt, ck, sk, cqt, sqt, *, tm):
    B, S, D = x.shape
    hd = wk.shape[1]
    n_pairs = hd // LANES
    nb = S // MOBA_BLOCK
    kernel = functools.partial(_qkv_moba_kernel, tm=tm, n_pairs=n_pairs)
    bpt = tm // MOBA_BLOCK
    return pl.pallas_call(
        kernel,
        grid=(B, S // tm),
        in_specs=[
            pl.BlockSpec((1, tm, D), lambda b, t: (b, t, 0)),
            _resident((D, hd), lambda b, t: (0, 0)),
            _resident((hd, D), lambda b, t: (0, 0)),
            _resident((hd, D), lambda b, t: (0, 0)),
            pl.BlockSpec((tm, LANES), lambda b, t: (t, 0)),
            pl.BlockSpec((tm, LANES), lambda b, t: (t, 0)),
            pl.BlockSpec((LANES, tm), lambda b, t: (0, t)),
            pl.BlockSpec((LANES, tm), lambda b, t: (0, t)),
        ],
        out_specs=[
            pl.BlockSpec((1, tm, hd), lambda b, t: (b, t, 0)),
            pl.BlockSpec((1, n_pairs, bpt, LANES, MOBA_BLOCK), lambda b, t: (b, 0, t, 0, 0)),
            pl.BlockSpec((1, n_pairs, bpt, LANES, MOBA_BLOCK), lambda b, t: (b, 0, t, 0, 0)),
            pl.BlockSpec((1, 1, bpt, hd), lambda b, t: (b, t, 0, 0)),
        ],
        out_shape=[
            jax.ShapeDtypeStruct((B, S, hd), BF16),
            jax.ShapeDtypeStruct((B, n_pairs, nb, LANES, MOBA_BLOCK), BF16),
            jax.ShapeDtypeStruct((B, n_pairs, nb, LANES, MOBA_BLOCK), BF16),
            jax.ShapeDtypeStruct((B, S // tm, bpt, hd), F32),
        ],
        compiler_params=_compiler_params(("parallel", "parallel")),
        name="qkv_moba",
    )(x, wk, wqt, wvt, ck, sk, cqt, sqt)


def _top3_blocks(gate, n_past):
    nb = gate.shape[0]
    row_i = lax.broadcasted_iota(jnp.int32, gate.shape, 0)
    row = row_i.astype(F32)
    g = jnp.where(row_i < n_past, gate, NEG_INF)
    sel = jnp.zeros(gate.shape, F32)
    for _ in range(MOBA_TOPK):
        mx = jnp.max(g, axis=0, keepdims=True)
        is_mx = jnp.logical_and(g == mx, g > NEG_INF)
        first = jnp.min(jnp.where(is_mx, row, nb), axis=0, keepdims=True)
        pick = row == first
        sel = jnp.where(pick, 1.0, sel)
        g = jnp.where(pick, NEG_INF, g)
    return sel


def _moba_kernel(qt_ref, k_ref, vt_ref, kmean_ref, o_ref, sel_ref, acc_ref, merged_ref, *, nb, unroll, pairs, lookahead):
    blk = MOBA_BLOCK
    n_heads = 2 * pairs
    n_items = nb + 1
    kmean = kmean_ref[0].astype(BF16)
    zeros_half = jnp.zeros((HALF_DIM, blk), BF16)
    ones_rows = jnp.ones((BF16_SUBLANES, blk), BF16)
    key_idx = lax.broadcasted_iota(jnp.int32, (blk, blk), 0)
    qry_idx = lax.broadcasted_iota(jnp.int32, (blk, blk), 1)
    causal = key_idx <= qry_idx

    def q_head(tile, h):
        qt = qt_ref[0, h // 2, tile]
        lo = (h % 2) * HALF_DIM
        rows = [zeros_half] * (4 * pairs)
        rows[4 * (h // 2) + h % 2] = qt[lo:lo + HALF_DIM]
        rows[4 * (h // 2) + 2 + h % 2] = qt[HEAD_DIM + lo:HEAD_DIM + lo + HALF_DIM]
        return jnp.concatenate(rows, axis=0)

    def values_aug(j, h):
        vt_pair = vt_ref[0, h // 2, j]
        return jnp.concatenate([vt_pair[(h % 2) * HEAD_DIM:(h % 2 + 1) * HEAD_DIM], ones_rows], axis=0)

    def pair_tiles(pair):
        return pair, nb - 1 - pair

    def choose(pair):
        tiles = pair_tiles(pair)
        return {(x, h): _top3_blocks(jnp.dot(kmean, q_head(tiles[x], h), preferred_element_type=F32), tiles[x])
                for x in range(2) for h in range(n_heads)}

    def write_merged(pair):
        tiles = pair_tiles(pair)
        for x in range(2):
            outs = []
            for h in range(n_heads):
                acc = merged_ref[x, h]
                outs.append(acc[:HEAD_DIM] * (1.0 / acc[HEAD_DIM:HEAD_DIM + 1]))
            o_all = jnp.concatenate(outs, axis=0)
            o_ref[0, pl.ds(pl.multiple_of(tiles[x] * blk, blk), blk), :] = o_all.T.astype(BF16)

    n_pairs = nb // 2
    acc_ref[...] = jnp.ones(acc_ref.shape, F32)
    merged_ref[...] = jnp.ones(merged_ref.shape, F32)
    for key, sel in choose(0).items():
        sel_ref[(0,) + key] = sel

    def tile_pair(a, carry):
        slot = a % 2
        tiles = pair_tiles(a)

        def item(t):
            if t < 2:
                return t == 0, t, tiles[t], tiles[t], True
            past = t - 2
            first = past < a
            return first, jnp.where(first, 0, 1), jnp.where(first, a, nb - 1 - a), jnp.where(first, past, past - a), False

        def chain_scores(t, h):
            _, _, tile, block, _ = item(t)
            k_j = k_ref[0, pl.ds(pl.multiple_of(block * blk, blk), blk), :]
            return jnp.dot(k_j, q_head(tile, h), preferred_element_type=F32)

        neg = jnp.full((1, blk), NEG_INF, F32)
        m_run = {(x, h, u): neg for x in range(2) for h in range(n_heads) for u in range(unroll)}

        n_chains = n_items * n_heads
        next_sel = None
        scores = {j: chain_scores(j // n_heads, j % n_heads) for j in range(lookahead)}
        for j in range(n_chains):
            if j == n_chains // 8:
                next_sel = choose(jnp.minimum(a + 1, n_pairs - 1))
            if j == n_chains // 4:
                write_merged(jnp.maximum(a - 1, 0))
            t, h = j // n_heads, j % n_heads
            u = t % unroll
            first, x, _, block, own = item(t)
            s = scores.pop(j)
            if j + lookahead < n_chains:
                scores[j + lookahead] = chain_scores((j + lookahead) // n_heads, (j + lookahead) % n_heads)
            if own:
                s = jnp.where(causal, s, NEG_INF)
                m_old = m_run[(x, h, u)]
                m_new = jnp.maximum(m_old, jnp.max(s, axis=0, keepdims=True))
                m_run[(x, h, u)] = m_new
                shift = m_new
            else:
                chosen = sel_ref[slot, x, h, pl.ds(block, 1), :] > 0.0
                m_old = jnp.where(first, m_run[(0, h, u)], m_run[(1, h, u)])
                m_new = jnp.maximum(m_old, jnp.where(chosen, jnp.max(s, axis=0, keepdims=True), NEG_INF))
                m_run[(0, h, u)] = jnp.where(first, m_new, m_run[(0, h, u)])
                m_run[(1, h, u)] = jnp.where(first, m_run[(1, h, u)], m_new)
                m_new = jnp.where(m_new == NEG_INF, 0.0, m_new)
                shift = jnp.where(chosen, m_new, POS_INF)
            p = jnp.exp2(s - shift).astype(BF16)
            pv = jnp.dot(values_aug(block, h), p, preferred_element_type=F32)
            acc_ref[x, h, u] = acc_ref[x, h, u] * jnp.exp2(m_old - m_new) + pv

        for x in range(2):
            for h in range(n_heads):
                ms = [m_run[(x, h, u)] for u in range(unroll)]
                m_all = functools.reduce(jnp.maximum, ms)
                merged_ref[x, h] = functools.reduce(
                    jnp.add, [acc_ref[x, h, u] * jnp.exp2(ms[u] - m_all) for u in range(unroll)])
        for key, sel in next_sel.items():
            sel_ref[(1 - slot,) + key] = sel
        return carry

    lax.fori_loop(0, n_pairs, tile_pair, 0)
    write_merged(n_pairs - 1)


def _moba_attention(qt, k, vt, kmean, *, unroll, pairs, lookahead):
    B, n_pairs, nb, _, blk = qt.shape
    S = k.shape[1]
    hd = k.shape[2]
    assert nb % 2 == 0 and n_pairs % pairs == 0
    n_heads = 2 * pairs
    width = pairs * LANES
    kernel = functools.partial(_moba_kernel, nb=nb, unroll=unroll, pairs=pairs, lookahead=lookahead)
    return pl.pallas_call(
        kernel,
        grid=(B, n_pairs // pairs),
        in_specs=[
            pl.BlockSpec((1, pairs, nb, LANES, blk), lambda b, p: (b, p, 0, 0, 0)),
            pl.BlockSpec((1, S, width), lambda b, p: (b, 0, p)),
            pl.BlockSpec((1, pairs, nb, LANES, blk), lambda b, p: (b, p, 0, 0, 0)),
            pl.BlockSpec((1, nb, width), lambda b, p: (b, 0, p)),
        ],
        out_specs=pl.BlockSpec((1, S, width), lambda b, p: (b, 0, p)),
        out_shape=jax.ShapeDtypeStruct((B, S, hd), BF16),
        scratch_shapes=[
            pltpu.VMEM((2, 2, n_heads, nb, blk), F32),
            pltpu.VMEM((2, n_heads, unroll, HEAD_DIM + BF16_SUBLANES, blk), F32),
            pltpu.VMEM((2, n_heads, HEAD_DIM + BF16_SUBLANES, blk), F32),
        ],
        compiler_params=_compiler_params(("parallel", "parallel")),
        name="moba_attention",
    )(qt, k, vt, kmean)


def _qkv_dil_kernel(x_ref, w_ref, cq_ref, sq_ref, ck_ref, sk_ref, *refs, hd, dilations, tm):
    out_refs, scr_ref = refs[:-1], refs[-1]
    gw = hd // len(dilations)
    xb = x_ref[...].astype(BF16)
    qkv = jnp.dot(xb, w_ref[...], preferred_element_type=F32)
    slot = 0
    for g, d in enumerate(dilations):
        for kind in range(3):
            ref = out_refs[3 * g + kind]
            for slab in range(gw // LANES):
                col = kind * hd + g * gw + slab * LANES
                t = qkv[:, col:col + LANES]
                if kind == 0:
                    t = t * cq_ref[...] + pltpu.roll(t, HEAD_DIM, axis=1) * sq_ref[...]
                elif kind == 1:
                    t = t * ck_ref[...] + pltpu.roll(t, HEAD_DIM, axis=1) * sk_ref[...]
                lanes = slice(slab * LANES, (slab + 1) * LANES)
                if d == 1:
                    ref[0, 0, :, lanes] = t.astype(BF16)
                else:
                    scr_ref[slot] = t
                    for r in range(d):
                        ref[0, r, :, lanes] = scr_ref[slot, pl.ds(r, tm // d, stride=d), :].astype(BF16)
                    slot += 1


def _qkv_dil(x2d, w, cq, sq, ck, sk, *, tm, batch, seq, dilations):
    n, d_model = x2d.shape
    hd = w.shape[1] // 3
    gw = hd // len(dilations)
    tiles_per_seq = seq // tm
    kernel = functools.partial(_qkv_dil_kernel, hd=hd, dilations=dilations, tm=tm)
    tab = pl.BlockSpec((tm, LANES), lambda t: (t % tiles_per_seq, 0))
    out_specs, out_shape = [], []
    for d in dilations:
        for _ in range(3):
            out_specs.append(pl.BlockSpec((1, d, tm // d, gw),
                                          lambda t: (t // tiles_per_seq, 0, t % tiles_per_seq, 0)))
            out_shape.append(jax.ShapeDtypeStruct((batch, d, seq // d, gw), BF16))
    n_slots = sum(3 * (gw // LANES) for d in dilations if d > 1)
    return pl.pallas_call(
        kernel,
        grid=(n // tm,),
        in_specs=[pl.BlockSpec((tm, d_model), lambda t: (t, 0)),
                  _resident((d_model, 3 * hd), lambda t: (0, 0)),
                  tab, tab, tab, tab],
        out_specs=out_specs,
        out_shape=out_shape,
        scratch_shapes=[pltpu.VMEM((max(n_slots, 1), tm, LANES), F32)],
        compiler_params=_compiler_params(("parallel",)),
        name="qkv_dilated",
    )(x2d, w, cq, sq, ck, sk)


def _dilated_kernel(q_ref, kc_ref, kh_ref, vc_ref, vh_ref, o_ref, lse_ref, *, tl, span):
    wb = DIL_BLOCK
    n_blk = pl.program_id(2)
    q = q_ref[...]
    k_all = jnp.concatenate([kh_ref[...], kc_ref[...]], axis=0)
    v_all = jnp.concatenate([vh_ref[...], vc_ref[...]], axis=0)
    lane = lax.broadcasted_iota(jnp.int32, (wb, LANES), 1)
    first_head_lanes = (lane % HEAD_DIM) < HALF_DIM
    first_head_out = lane < HEAD_DIM
    qq = lax.broadcasted_iota(jnp.int32, (wb, 2 * wb), 0)
    kk = lax.broadcasted_iota(jnp.int32, (wb, 2 * wb), 1)
    dist = qq + wb - kk
    band = jnp.logical_and(dist >= 0, dist <= span)
    nt_dims = (((1,), (1,)), ((), ()))
    units = [(pair, sub, h) for pair in range(q.shape[1] // LANES) for sub in range(tl // wb) for h in range(2)]

    def unit_scores(pair, sub, h):
        lo, hi = pair * LANES, (pair + 1) * LANES
        qs = q[sub * wb:(sub + 1) * wb, lo:hi]
        ks = k_all[sub * wb:sub * wb + 2 * wb, lo:hi]
        keep = first_head_lanes if h == 0 else jnp.logical_not(first_head_lanes)
        qm = jnp.where(keep, qs, jnp.zeros_like(qs))
        return lax.dot_general(qm, ks, nt_dims, preferred_element_type=F32)

    ahead = 3
    scores = {n: unit_scores(*units[n]) for n in range(ahead)}
    outs, lses = {}, {}
    for n, (pair, sub, h) in enumerate(units):
        lo, hi = pair * LANES, (pair + 1) * LANES
        s = scores.pop(n)
        if n + ahead < len(units):
            scores[n + ahead] = unit_scores(*units[n + ahead])
        mask = jnp.logical_and(band, jnp.logical_or(kk >= wb, n_blk > 0)) if sub == 0 else band
        s = jnp.where(mask, s, NEG_INF)
        m = jnp.max(s, axis=1, keepdims=True)
        p = jnp.exp2(s - m)
        l = jnp.sum(p, axis=1, keepdims=True)
        vs = v_all[sub * wb:sub * wb + 2 * wb, lo:hi]
        pv = jnp.dot(p.astype(BF16), vs, preferred_element_type=F32)
        outs[h] = pv * (1.0 / l)
        lses[h] = m + jnp.log2(l)
        if h == 1:
            rows = slice(sub * wb, (sub + 1) * wb)
            o_ref[rows, lo:hi] = jnp.where(first_head_out, outs[0], outs[1]).astype(BF16)
            lse_ref[rows, lo:hi] = jnp.where(
                first_head_out, jnp.broadcast_to(lses[0], (wb, LANES)), jnp.broadcast_to(lses[1], (wb, LANES)))


def _dilated_group(q, k, v, *, group, window, dilation):
    batch, _, L, gw = q.shape
    span = window // dilation
    assert span <= DIL_BLOCK and L % DIL_BLOCK == 0
    tl = min(8 * DIL_BLOCK, L)
    halo_per_tile = tl // DIL_BLOCK
    cur = pl.BlockSpec((None, None, tl, gw), lambda b, r, n: (b, r, n, 0))
    halo = pl.BlockSpec((None, None, DIL_BLOCK, gw),
                        lambda b, r, n: (b, r, jnp.maximum(n * halo_per_tile - 1, 0), 0))
    kernel = functools.partial(_dilated_kernel, tl=tl, span=span)
    return pl.pallas_call(
        kernel,
        grid=(batch, dilation, L // tl),
        in_specs=[cur, cur, halo, cur, halo],
        out_specs=[cur, cur],
        out_shape=[jax.ShapeDtypeStruct(q.shape, BF16), jax.ShapeDtypeStruct(q.shape, F32)],
        compiler_params=_compiler_params(("parallel", "parallel", "arbitrary")),
        name=f"dilated_attention_g{group}",
    )(q, k, k, v, v)


def _layer_norm(y, g, b):
    mu = jnp.mean(y, axis=-1, keepdims=True)
    yc = y - mu
    var = jnp.mean(yc * yc, axis=-1, keepdims=True)
    return yc * lax.rsqrt(var + LN_EPS) * g + b


def _position_order(ref, d, scr_ref, slot, tm):
    if d == 1:
        return ref[0, 0]
    n_slabs = ref.shape[-1] // LANES
    for slab in range(n_slabs):
        for r in range(d):
            scr_ref[slot + slab, pl.ds(r, tm // d, stride=d), :] = (
                ref[0, r, :, slab * LANES:(slab + 1) * LANES].astype(F32))
    return jnp.concatenate([scr_ref[slot + slab] for slab in range(n_slabs)], axis=1)


def _tail_kernel(*refs, dilations, tm, ff_chunk):
    n_groups = len(dilations)
    n_mix = 2 * n_groups if n_groups > 1 else 1
    mix_refs = refs[:n_mix]
    (x_ref, wo_ref, g1_ref, b1_ref, win_ref, wout_ref, g2_ref, b2_ref, out_ref) = refs[n_mix:n_mix + 9]
    scr_ref = refs[n_mix + 9] if len(refs) > n_mix + 9 else None
    if n_groups == 1:
        o = _position_order(mix_refs[0], dilations[0], scr_ref, 0, tm)
    else:
        slabs = mix_refs[0].shape[-1] // LANES
        outs = [_position_order(mix_refs[g], d, scr_ref, 2 * g * slabs, tm).astype(F32)
                for g, d in enumerate(dilations)]
        lses = [_position_order(mix_refs[n_groups + g], d, scr_ref, (2 * g + 1) * slabs, tm)
                for g, d in enumerate(dilations)]
        top = functools.reduce(jnp.maximum, lses)
        ws = [jnp.exp2(l - top) for l in lses]
        inv = 1.0 / functools.reduce(jnp.add, ws)
        o = jnp.concatenate([(outs[g] * (ws[g] * inv)).astype(BF16) for g in range(n_groups)], axis=1)
    d_ff = win_ref.shape[1]
    rows = tm // TAIL_SUBTILES

    def mlp(hb):
        acc = jnp.zeros(hb.shape, F32)
        for c in range(d_ff // ff_chunk):
            lo, hi = c * ff_chunk, (c + 1) * ff_chunk
            hid = jnp.maximum(jnp.dot(hb, win_ref[:, lo:hi], preferred_element_type=F32), 0.0)
            acc = acc + jnp.dot((hid * hid).astype(BF16), wout_ref[lo:hi, :], preferred_element_type=F32)
        return acc

    sub = [slice(r * rows, (r + 1) * rows) for r in range(TAIL_SUBTILES)]
    mixes = [jnp.dot(o[rs], wo_ref[...], preferred_element_type=F32) for rs in sub]
    hs, mlps = [], []
    for r, rs in enumerate(sub):
        hs.append(_layer_norm(DEEPNORM_ALPHA * x_ref[rs, :] + mixes[r], g1_ref[...], b1_ref[...]))
        mlps.append(mlp(hs[r].astype(BF16)))
    for r, rs in enumerate(sub):
        out_ref[rs, :] = _layer_norm(DEEPNORM_ALPHA * hs[r] + mlps[r], g2_ref[...], b2_ref[...])


def _layer_tail(mix_inputs, dilations, x2d, wo, g1, b1, win, wout, g2, b2, *, tm, seq):
    n, d_model = x2d.shape
    d_ff = win.shape[1]
    tiles_per_seq = seq // tm
    n_groups = len(dilations)

    def mix_spec(a, d):
        return pl.BlockSpec((1, d, tm // d, a.shape[-1]),
                            lambda t: (t // tiles_per_seq, 0, t % tiles_per_seq, 0))

    vec = _resident((1, d_model), lambda t: (0, 0))
    kernel = functools.partial(_tail_kernel, dilations=dilations, tm=tm, ff_chunk=min(1024, d_ff))
    slabs = mix_inputs[0].shape[-1] // LANES
    n_slots = 2 * n_groups * slabs if any(d > 1 for d in dilations) else 0
    return pl.pallas_call(
        kernel,
        grid=(n // tm,),
        in_specs=[mix_spec(a, dilations[i % n_groups]) for i, a in enumerate(mix_inputs)] + [
            pl.BlockSpec((tm, d_model), lambda t: (t, 0)),
            _resident(wo.shape, lambda t: (0, 0)),
            vec, vec,
            _resident(win.shape, lambda t: (0, 0)),
            _resident(wout.shape, lambda t: (0, 0)),
            vec, vec,
        ],
        out_specs=pl.BlockSpec((tm, d_model), lambda t: (t, 0)),
        out_shape=jax.ShapeDtypeStruct((n, d_model), F32),
        scratch_shapes=[pltpu.VMEM((n_slots, tm, LANES), F32)] if n_slots else [],
        compiler_params=_compiler_params(("parallel",)),
        name=f"layer_tail_{n_groups}",
    )(*mix_inputs, x2d, wo, g1, b1, win, wout, g2, b2)


def kernel(x, moba_w_qkv, moba_w_o, dil_w_qkv, dil_w_o, mlp_w_in, mlp_w_out,
           ln_mix_g, ln_mix_b, ln_mlp_g, ln_mlp_b):
    B, S, D = x.shape
    n_tok = B * S
    tm = min(512, S)
    q_scale = HEAD_DIM ** -0.5 * LOG2E
    cq, sq = _rotary_tables(S, q_scale)
    ck, sk = _rotary_tables(S, 1.0)
    vec = lambda a: a.reshape(1, D).astype(F32)

    w = moba_w_qkv[0]
    hd = w.shape[1] // 3
    wq = _pair_heads(w[:, :hd])
    wk = _pair_heads(w[:, hd:2 * hd])
    wv = w[:, 2 * hd:]
    k, qt, vt, kmean = _qkv_moba(x, wk.astype(BF16), wq.T.astype(BF16), wv.T.astype(BF16),
                                 ck, sk, *_rotary_tables(S, q_scale, transposed=True), tm=min(1024, S))
    attn = _moba_attention(qt, k, vt, kmean.reshape(B, S // MOBA_BLOCK, hd), unroll=4, pairs=1, lookahead=5)
    h = _layer_tail([attn.reshape(B, 1, S, hd)], (1,), x.reshape(n_tok, D), moba_w_o[0].astype(BF16),
                    vec(ln_mix_g[0]), vec(ln_mix_b[0]), mlp_w_in[0].astype(BF16), mlp_w_out[0].astype(BF16),
                    vec(ln_mlp_g[0]), vec(ln_mlp_b[0]), tm=tm, seq=S)

    w = dil_w_qkv[0]
    hd = w.shape[1] // 3
    w_all = jnp.concatenate([_pair_heads(w[:, :hd]), _pair_heads(w[:, hd:2 * hd]), w[:, 2 * hd:]], axis=1)
    dilations = tuple(d for _, d in DIL_GROUPS)
    qkv = _qkv_dil(h, w_all.astype(BF16), cq, sq, ck, sk, tm=min(1024, S), batch=B, seq=S, dilations=dilations)
    outs, lses = [], []
    for g, (window, dilation) in enumerate(DIL_GROUPS):
        o_g, lse_g = _dilated_group(*qkv[3 * g:3 * g + 3], group=g, window=window, dilation=dilation)
        outs.append(o_g)
        lses.append(lse_g)
    h = _layer_tail(outs + lses, dilations, h, dil_w_o[0].astype(BF16),
                    vec(ln_mix_g[1]), vec(ln_mix_b[1]), mlp_w_in[1].astype(BF16), mlp_w_out[1].astype(BF16),
                    vec(ln_mlp_g[1]), vec(ln_mlp_b[1]), tm=tm, seq=S)
    return h.reshape(B, S, D)
```

```python
import functools
import math

import jax
import jax.numpy as jnp
from jax import lax
from jax.experimental import pallas as pl
from jax.experimental.pallas import tpu as pltpu

HEAD_DIM = 64
HALF_DIM = HEAD_DIM // 2
ROPE_THETA = 10000.0
ROPE_SPLIT = 256
LN_EPS = 1e-5
DEPTH = 2
DEEPNORM_ALPHA = (2.0 * DEPTH) ** 0.25
MOBA_BLOCK = 256
MOBA_TOPK = 3
DIL_GROUPS = ((128, 1), (512, 4), (2048, 16))
DIL_BLOCK = 128

LANES = 128
BF16_SUBLANES = 16
VMEM_LIMIT_BYTES = 56 * 1024 * 1024

PROJ_ROW_TILE = 1024
TAIL_ROW_TILE = 512
TAIL_SUBTILES = 2
TAIL_FF_CHUNK = 1024
MOBA_CHAINS = 4
MOBA_LOOKAHEAD = 5
DIL_ROW_TILE = 8 * DIL_BLOCK
DIL_LOOKAHEAD = 3

LOG2E = math.log2(math.e)

F32 = jnp.float32
BF16 = jnp.bfloat16
NEG_INF = float("-inf")
POS_INF = float("inf")


def _compiler_params(semantics):
    return pltpu.CompilerParams(dimension_semantics=semantics,
                                vmem_limit_bytes=VMEM_LIMIT_BYTES)


def _resident(block_shape, index_map):
    return pl.BlockSpec(block_shape, index_map, pipeline_mode=pl.Buffered(1))


def _pair_heads(w):
    d, hd = w.shape
    w = w.reshape(d, hd // LANES, 2, 2, HALF_DIM)
    return w.transpose(0, 1, 3, 2, 4).reshape(d, hd)


def _rotary_tables(seq, scale, transposed=False):
    inv = 1.0 / (ROPE_THETA ** (jnp.arange(0, HEAD_DIM, 2, dtype=F32) / HEAD_DIM))
    inv = jnp.tile(inv, LANES // HALF_DIM)
    sign = jnp.where(jnp.arange(LANES) < HEAD_DIM, -scale, scale).astype(F32)
    coarse = jnp.arange(0, seq, ROPE_SPLIT, dtype=F32)
    fine = jnp.arange(ROPE_SPLIT, dtype=F32)
    if transposed:
        ang_a = (inv[:, None] * coarse[None, :])[:, :, None]
        ang_b = (inv[:, None] * fine[None, :])[:, None, :]
        sign, shape = sign[:, None], (LANES, seq)
    else:
        ang_a = (coarse[:, None] * inv[None, :])[:, None, :]
        ang_b = (fine[:, None] * inv[None, :])[None, :, :]
        sign, shape = sign[None, :], (seq, LANES)
    cos = (jnp.cos(ang_a) * jnp.cos(ang_b) - jnp.sin(ang_a) * jnp.sin(ang_b)).reshape(shape)
    sin = (jnp.sin(ang_a) * jnp.cos(ang_b) + jnp.cos(ang_a) * jnp.sin(ang_b)).reshape(shape)
    return cos * scale, sin * sign


def _qkv_moba_kernel(x_ref, wk_ref, wqt_ref, wvt_ref, ck_ref, sk_ref, cqt_ref, sqt_ref,
                     k_ref, qt_ref, vt_ref, kmean_ref, *, tm, n_pairs):
    blocks_per_tile = tm // MOBA_BLOCK
    xb = x_ref[0].astype(BF16)

    k = jnp.dot(xb, wk_ref[...], preferred_element_type=F32)
    ck = ck_ref[...]
    sk = sk_ref[...]
    for g in range(n_pairs):
        t = k[:, g * LANES:(g + 1) * LANES]
        kr = t * ck + pltpu.roll(t, HEAD_DIM, axis=1) * sk
        k_ref[0, :, g * LANES:(g + 1) * LANES] = kr.astype(BF16)
        for blk in range(blocks_per_tile):
            mean = jnp.mean(kr[blk * MOBA_BLOCK:(blk + 1) * MOBA_BLOCK], axis=0, keepdims=True)
            kmean_ref[0, 0, blk:blk + 1, g * LANES:(g + 1) * LANES] = mean

    nt_dims = (((1,), (1,)), ((), ()))
    qt = lax.dot_general(wqt_ref[...], xb, nt_dims, preferred_element_type=F32)
    cq = cqt_ref[...]
    sq = sqt_ref[...]
    for g in range(n_pairs):
        t = qt[g * LANES:(g + 1) * LANES]
        swapped = jnp.concatenate([t[HEAD_DIM:], t[:HEAD_DIM]], axis=0)
        qr = (t * cq + swapped * sq).astype(BF16)
        for j in range(blocks_per_tile):
            qt_ref[0, g, j] = qr[:, j * MOBA_BLOCK:(j + 1) * MOBA_BLOCK]

    vt = lax.dot_general(wvt_ref[...], xb, nt_dims, preferred_element_type=F32).astype(BF16)
    for g in range(n_pairs):
        for j in range(blocks_per_tile):
            vt_ref[0, g, j] = vt[g * LANES:(g + 1) * LANES, j * MOBA_BLOCK:(j + 1) * MOBA_BLOCK]


def _qkv_moba(x, wk, wqt, wvt, ck, sk, cqt, sqt, *, tm):
    B, S, D = x.shape
    hd = wk.shape[1]
    n_pairs = hd // LANES
    nb = S // MOBA_BLOCK
    kernel = functools.partial(_qkv_moba_kernel, tm=tm, n_pairs=n_pairs)
    bpt = tm // MOBA_BLOCK
    return pl.pallas_call(
        kernel,
        grid=(B, S // tm),
        in_specs=[
            pl.BlockSpec((1, tm, D), lambda b, t: (b, t, 0)),
            _resident((D, hd), lambda b, t: (0, 0)),
            _resident((hd, D), lambda b, t: (0, 0)),
            _resident((hd, D), lambda b, t: (0, 0)),
            pl.BlockSpec((tm, LANES), lambda b, t: (t, 0)),
            pl.BlockSpec((tm, LANES), lambda b, t: (t, 0)),
            pl.BlockSpec((LANES, tm), lambda b, t: (0, t)),
            pl.BlockSpec((LANES, tm), lambda b, t: (0, t)),
        ],
        out_specs=[
            pl.BlockSpec((1, tm, hd), lambda b, t: (b, t, 0)),
            pl.BlockSpec((1, n_pairs, bpt, LANES, MOBA_BLOCK), lambda b, t: (b, 0, t, 0, 0)),
            pl.BlockSpec((1, n_pairs, bpt, LANES, MOBA_BLOCK), lambda b, t: (b, 0, t, 0, 0)),
            pl.BlockSpec((1, 1, bpt, hd), lambda b, t: (b, t, 0, 0)),
        ],
        out_shape=[
            jax.ShapeDtypeStruct((B, S, hd), BF16),
            jax.ShapeDtypeStruct((B, n_pairs, nb, LANES, MOBA_BLOCK), BF16),
            jax.ShapeDtypeStruct((B, n_pairs, nb, LANES, MOBA_BLOCK), BF16),
            jax.ShapeDtypeStruct((B, S // tm, bpt, hd), F32),
        ],
        compiler_params=_compiler_params(("parallel", "parallel")),
        name="qkv_moba",
    )(x, wk, wqt, wvt, ck, sk, cqt, sqt)


def _top3_blocks(gate, n_past):
    nb = gate.shape[0]
    row_i = lax.broadcasted_iota(jnp.int32, gate.shape, 0)
    row = row_i.astype(F32)
    g = jnp.where(row_i < n_past, gate, NEG_INF)
    sel = jnp.zeros(gate.shape, F32)
    for _ in range(MOBA_TOPK):
        mx = jnp.max(g, axis=0, keepdims=True)
        is_mx = jnp.logical_and(g == mx, g > NEG_INF)
        first = jnp.min(jnp.where(is_mx, row, nb), axis=0, keepdims=True)
        pick = row == first
        sel = jnp.where(pick, 1.0, sel)
        g = jnp.where(pick, NEG_INF, g)
    return sel


def _moba_kernel(qt_ref, k_ref, vt_ref, kmean_ref, o_ref, sel_ref, acc_ref, merged_ref, *,
                 nb, chains, pairs, lookahead):
    blk = MOBA_BLOCK
    n_heads = 2 * pairs
    n_items = nb + 1
    kmean = kmean_ref[0].astype(BF16)
    zeros_half = jnp.zeros((HALF_DIM, blk), BF16)
    ones_rows = jnp.ones((BF16_SUBLANES, blk), BF16)
    key_idx = lax.broadcasted_iota(jnp.int32, (blk, blk), 0)
    qry_idx = lax.broadcasted_iota(jnp.int32, (blk, blk), 1)
    causal = key_idx <= qry_idx

    def q_head(tile, h):
        qt = qt_ref[0, h // 2, tile]
        lo = (h % 2) * HALF_DIM
        rows = [zeros_half] * (4 * pairs)
        rows[4 * (h // 2) + h % 2] = qt[lo:lo + HALF_DIM]
        rows[4 * (h // 2) + 2 + h % 2] = qt[HEAD_DIM + lo:HEAD_DIM + lo + HALF_DIM]
        return jnp.concatenate(rows, axis=0)

    def values_aug(j, h):
        vt_pair = vt_ref[0, h // 2, j]
        return jnp.concatenate([vt_pair[(h % 2) * HEAD_DIM:(h % 2 + 1) * HEAD_DIM], ones_rows], axis=0)

    def pair_tiles(pair):
        return pair, nb - 1 - pair

    def choose(pair):
        tiles = pair_tiles(pair)
        return {(x, h): _top3_blocks(jnp.dot(kmean, q_head(tiles[x], h), preferred_element_type=F32), tiles[x])
                for x in range(2) for h in range(n_heads)}

    def write_merged(pair):
        tiles = pair_tiles(pair)
        for x in range(2):
            outs = []
            for h in range(n_heads):
                acc = merged_ref[x, h]
                outs.append(acc[:HEAD_DIM] * (1.0 / acc[HEAD_DIM:HEAD_DIM + 1]))
            o_all = jnp.concatenate(outs, axis=0)
            o_ref[0, pl.ds(pl.multiple_of(tiles[x] * blk, blk), blk), :] = o_all.T.astype(BF16)

    n_pairs = nb // 2
    acc_ref[...] = jnp.ones(acc_ref.shape, F32)
    merged_ref[...] = jnp.ones(merged_ref.shape, F32)
    for key, sel in choose(0).items():
        sel_ref[(0,) + key] = sel

    def tile_pair(a, carry):
        slot = a % 2
        tiles = pair_tiles(a)

        def item(t):
            if t < 2:
                return t == 0, t, tiles[t], tiles[t], True
            past = t - 2
            first = past < a
            slot_x = jnp.where(first, 0, 1)
            return first, slot_x, jnp.where(first, a, nb - 1 - a), jnp.where(first, past, past - a), False

        def chain_scores(t, h):
            _, _, tile, block, _ = item(t)
            k_j = k_ref[0, pl.ds(pl.multiple_of(block * blk, blk), blk), :]
            return jnp.dot(k_j, q_head(tile, h), preferred_element_type=F32)

        neg = jnp.full((1, blk), NEG_INF, F32)
        m_run = {(x, h, u): neg for x in range(2) for h in range(n_heads) for u in range(chains)}

        n_chains = n_items * n_heads
        next_sel = None
        scores = {j: chain_scores(j // n_heads, j % n_heads) for j in range(lookahead)}
        for j in range(n_chains):
            if j == n_chains // 8:
                next_sel = choose(jnp.minimum(a + 1, n_pairs - 1))
            if j == n_chains // 4:
                write_merged(jnp.maximum(a - 1, 0))
            t, h = j // n_heads, j % n_heads
            u = t % chains
            first, x, _, block, own = item(t)
            s = scores.pop(j)
            if j + lookahead < n_chains:
                scores[j + lookahead] = chain_scores((j + lookahead) // n_heads, (j + lookahead) % n_heads)
            if own:
                s = jnp.where(causal, s, NEG_INF)
                m_old = m_run[(x, h, u)]
                m_new = jnp.maximum(m_old, jnp.max(s, axis=0, keepdims=True))
                m_run[(x, h, u)] = m_new
                shift = m_new
            else:
                chosen = sel_ref[slot, x, h, pl.ds(block, 1), :] > 0.0
                m_old = jnp.where(first, m_run[(0, h, u)], m_run[(1, h, u)])
                m_new = jnp.maximum(m_old, jnp.where(chosen, jnp.max(s, axis=0, keepdims=True), NEG_INF))
                m_run[(0, h, u)] = jnp.where(first, m_new, m_run[(0, h, u)])
                m_run[(1, h, u)] = jnp.where(first, m_run[(1, h, u)], m_new)
                m_new = jnp.where(m_new == NEG_INF, 0.0, m_new)
                shift = jnp.where(chosen, m_new, POS_INF)
            p = jnp.exp2(s - shift).astype(BF16)
            pv = jnp.dot(values_aug(block, h), p, preferred_element_type=F32)
            acc_ref[x, h, u] = acc_ref[x, h, u] * jnp.exp2(m_old - m_new) + pv

        for x in range(2):
            for h in range(n_heads):
                ms = [m_run[(x, h, u)] for u in range(chains)]
                m_all = functools.reduce(jnp.maximum, ms)
                merged_ref[x, h] = functools.reduce(
                    jnp.add, [acc_ref[x, h, u] * jnp.exp2(ms[u] - m_all) for u in range(chains)])
        for key, sel in next_sel.items():
            sel_ref[(1 - slot,) + key] = sel
        return carry

    lax.fori_loop(0, n_pairs, tile_pair, 0)
    write_merged(n_pairs - 1)


def _moba_attention(qt, k, vt, kmean, *, chains, pairs, lookahead):
    B, n_pairs, nb, _, blk = qt.shape
    S = k.shape[1]
    hd = k.shape[2]
    assert nb % 2 == 0 and n_pairs % pairs == 0
    n_heads = 2 * pairs
    width = pairs * LANES
    kernel = functools.partial(_moba_kernel, nb=nb, chains=chains, pairs=pairs, lookahead=lookahead)
    return pl.pallas_call(
        kernel,
        grid=(B, n_pairs // pairs),
        in_specs=[
            pl.BlockSpec((1, pairs, nb, LANES, blk), lambda b, p: (b, p, 0, 0, 0)),
            pl.BlockSpec((1, S, width), lambda b, p: (b, 0, p)),
            pl.BlockSpec((1, pairs, nb, LANES, blk), lambda b, p: (b, p, 0, 0, 0)),
            pl.BlockSpec((1, nb, width), lambda b, p: (b, 0, p)),
        ],
        out_specs=pl.BlockSpec((1, S, width), lambda b, p: (b, 0, p)),
        out_shape=jax.ShapeDtypeStruct((B, S, hd), BF16),
        scratch_shapes=[
            pltpu.VMEM((2, 2, n_heads, nb, blk), F32),
            pltpu.VMEM((2, n_heads, chains, HEAD_DIM + BF16_SUBLANES, blk), F32),
            pltpu.VMEM((2, n_heads, HEAD_DIM + BF16_SUBLANES, blk), F32),
        ],
        compiler_params=_compiler_params(("parallel", "parallel")),
        name="moba_attention",
    )(qt, k, vt, kmean)


def _qkv_dil_kernel(x_ref, w_ref, cq_ref, sq_ref, ck_ref, sk_ref, *refs, hd, dilations, tm):
    out_refs, scr_ref = refs[:-1], refs[-1]
    gw = hd // len(dilations)
    xb = x_ref[...].astype(BF16)
    qkv = jnp.dot(xb, w_ref[...], preferred_element_type=F32)
    slot = 0
    for g, d in enumerate(dilations):
        for kind in range(3):
            ref = out_refs[3 * g + kind]
            for slab in range(gw // LANES):
                col = kind * hd + g * gw + slab * LANES
                t = qkv[:, col:col + LANES]
                if kind == 0:
                    t = t * cq_ref[...] + pltpu.roll(t, HEAD_DIM, axis=1) * sq_ref[...]
                elif kind == 1:
                    t = t * ck_ref[...] + pltpu.roll(t, HEAD_DIM, axis=1) * sk_ref[...]
                lanes = slice(slab * LANES, (slab + 1) * LANES)
                if d == 1:
                    ref[0, 0, :, lanes] = t.astype(BF16)
                else:
                    scr_ref[slot] = t
                    for r in range(d):
                        ref[0, r, :, lanes] = scr_ref[slot, pl.ds(r, tm // d, stride=d), :].astype(BF16)
                    slot += 1


def _qkv_dil(x2d, w, cq, sq, ck, sk, *, tm, batch, seq, dilations):
    n, d_model = x2d.shape
    hd = w.shape[1] // 3
    gw = hd // len(dilations)
    tiles_per_seq = seq // tm
    kernel = functools.partial(_qkv_dil_kernel, hd=hd, dilations=dilations, tm=tm)
    tab = pl.BlockSpec((tm, LANES), lambda t: (t % tiles_per_seq, 0))
    out_specs, out_shape = [], []
    for d in dilations:
        for _ in range(3):
            out_specs.append(pl.BlockSpec((1, d, tm // d, gw),
                                          lambda t: (t // tiles_per_seq, 0, t % tiles_per_seq, 0)))
            out_shape.append(jax.ShapeDtypeStruct((batch, d, seq // d, gw), BF16))
    n_slots = sum(3 * (gw // LANES) for d in dilations if d > 1)
    return pl.pallas_call(
        kernel,
        grid=(n // tm,),
        in_specs=[pl.BlockSpec((tm, d_model), lambda t: (t, 0)),
                  _resident((d_model, 3 * hd), lambda t: (0, 0)),
                  tab, tab, tab, tab],
        out_specs=out_specs,
        out_shape=out_shape,
        scratch_shapes=[pltpu.VMEM((max(n_slots, 1), tm, LANES), F32)],
        compiler_params=_compiler_params(("parallel",)),
        name="qkv_dilated",
    )(x2d, w, cq, sq, ck, sk)


def _dilated_kernel(q_ref, kc_ref, kh_ref, vc_ref, vh_ref, o_ref, lse_ref, *, tl, span):
    wb = DIL_BLOCK
    n_blk = pl.program_id(2)
    q = q_ref[...]
    k_all = jnp.concatenate([kh_ref[...], kc_ref[...]], axis=0)
    v_all = jnp.concatenate([vh_ref[...], vc_ref[...]], axis=0)
    lane = lax.broadcasted_iota(jnp.int32, (wb, LANES), 1)
    first_head_lanes = (lane % HEAD_DIM) < HALF_DIM
    first_head_out = lane < HEAD_DIM
    qq = lax.broadcasted_iota(jnp.int32, (wb, 2 * wb), 0)
    kk = lax.broadcasted_iota(jnp.int32, (wb, 2 * wb), 1)
    dist = qq + wb - kk
    band = jnp.logical_and(dist >= 0, dist <= span)
    nt_dims = (((1,), (1,)), ((), ()))
    units = [(pair, sub, h) for pair in range(q.shape[1] // LANES) for sub in range(tl // wb) for h in range(2)]

    def unit_scores(pair, sub, h):
        lo, hi = pair * LANES, (pair + 1) * LANES
        qs = q[sub * wb:(sub + 1) * wb, lo:hi]
        ks = k_all[sub * wb:sub * wb + 2 * wb, lo:hi]
        keep = first_head_lanes if h == 0 else jnp.logical_not(first_head_lanes)
        qm = jnp.where(keep, qs, jnp.zeros_like(qs))
        return lax.dot_general(qm, ks, nt_dims, preferred_element_type=F32)

    ahead = DIL_LOOKAHEAD
    scores = {n: unit_scores(*units[n]) for n in range(ahead)}
    outs, lses = {}, {}
    for n, (pair, sub, h) in enumerate(units):
        lo, hi = pair * LANES, (pair + 1) * LANES
        s = scores.pop(n)
        if n + ahead < len(units):
            scores[n + ahead] = unit_scores(*units[n + ahead])
        mask = jnp.logical_and(band, jnp.logical_or(kk >= wb, n_blk > 0)) if sub == 0 else band
        s = jnp.where(mask, s, NEG_INF)
        m = jnp.max(s, axis=1, keepdims=True)
        p = jnp.exp2(s - m)
        l = jnp.sum(p, axis=1, keepdims=True)
        vs = v_all[sub * wb:sub * wb + 2 * wb, lo:hi]
        pv = jnp.dot(p.astype(BF16), vs, preferred_element_type=F32)
        outs[h] = pv * (1.0 / l)
        lses[h] = m + jnp.log2(l)
        if h == 1:
            rows = slice(sub * wb, (sub + 1) * wb)
            o_ref[rows, lo:hi] = jnp.where(first_head_out, outs[0], outs[1]).astype(BF16)
            lse_ref[rows, lo:hi] = jnp.where(
                first_head_out, jnp.broadcast_to(lses[0], (wb, LANES)), jnp.broadcast_to(lses[1], (wb, LANES)))


def _dilated_group(q, k, v, *, group, window, dilation):
    batch, _, L, gw = q.shape
    span = window // dilation
    assert span <= DIL_BLOCK and L % DIL_BLOCK == 0
    tl = min(DIL_ROW_TILE, L)
    halo_per_tile = tl // DIL_BLOCK
    cur = pl.BlockSpec((None, None, tl, gw), lambda b, r, n: (b, r, n, 0))
    halo = pl.BlockSpec((None, None, DIL_BLOCK, gw),
                        lambda b, r, n: (b, r, jnp.maximum(n * halo_per_tile - 1, 0), 0))
    kernel = functools.partial(_dilated_kernel, tl=tl, span=span)
    return pl.pallas_call(
        kernel,
        grid=(batch, dilation, L // tl),
        in_specs=[cur, cur, halo, cur, halo],
        out_specs=[cur, cur],
        out_shape=[jax.ShapeDtypeStruct(q.shape, BF16), jax.ShapeDtypeStruct(q.shape, F32)],
        compiler_params=_compiler_params(("parallel", "parallel", "arbitrary")),
        name=f"dilated_attention_g{group}",
    )(q, k, k, v, v)


def _layer_norm(y, g, b):
    mu = jnp.mean(y, axis=-1, keepdims=True)
    yc = y - mu
    var = jnp.mean(yc * yc, axis=-1, keepdims=True)
    return yc * lax.rsqrt(var + LN_EPS) * g + b


def _position_order(ref, d, scr_ref, slot, tm):
    if d == 1:
        return ref[0, 0]
    n_slabs = ref.shape[-1] // LANES
    for slab in range(n_slabs):
        for r in range(d):
            scr_ref[slot + slab, pl.ds(r, tm // d, stride=d), :] = (
                ref[0, r, :, slab * LANES:(slab + 1) * LANES].astype(F32))
    return jnp.concatenate([scr_ref[slot + slab] for slab in range(n_slabs)], axis=1)


def _tail_kernel(*refs, dilations, tm, ff_chunk):
    n_groups = len(dilations)
    n_mix = 2 * n_groups if n_groups > 1 else 1
    mix_refs = refs[:n_mix]
    (x_ref, wo_ref, g1_ref, b1_ref, win_ref, wout_ref, g2_ref, b2_ref, out_ref) = refs[n_mix:n_mix + 9]
    scr_ref = refs[n_mix + 9] if len(refs) > n_mix + 9 else None
    if n_groups == 1:
        o = _position_order(mix_refs[0], dilations[0], scr_ref, 0, tm)
    else:
        slabs = mix_refs[0].shape[-1] // LANES
        outs = [_position_order(mix_refs[g], d, scr_ref, 2 * g * slabs, tm).astype(F32)
                for g, d in enumerate(dilations)]
        lses = [_position_order(mix_refs[n_groups + g], d, scr_ref, (2 * g + 1) * slabs, tm)
                for g, d in enumerate(dilations)]
        top = functools.reduce(jnp.maximum, lses)
        ws = [jnp.exp2(l - top) for l in lses]
        inv = 1.0 / functools.reduce(jnp.add, ws)
        o = jnp.concatenate([(outs[g] * (ws[g] * inv)).astype(BF16) for g in range(n_groups)], axis=1)
    d_ff = win_ref.shape[1]
    rows = tm // TAIL_SUBTILES

    def mlp(hb):
        acc = jnp.zeros(hb.shape, F32)
        for c in range(d_ff // ff_chunk):
            lo, hi = c * ff_chunk, (c + 1) * ff_chunk
            hid = jnp.maximum(jnp.dot(hb, win_ref[:, lo:hi], preferred_element_type=F32), 0.0)
            acc = acc + jnp.dot((hid * hid).astype(BF16), wout_ref[lo:hi, :], preferred_element_type=F32)
        return acc

    sub = [slice(r * rows, (r + 1) * rows) for r in range(TAIL_SUBTILES)]
    mixes = [jnp.dot(o[rs], wo_ref[...], preferred_element_type=F32) for rs in sub]
    hs, mlps = [], []
    for r, rs in enumerate(sub):
        hs.append(_layer_norm(DEEPNORM_ALPHA * x_ref[rs, :] + mixes[r], g1_ref[...], b1_ref[...]))
        mlps.append(mlp(hs[r].astype(BF16)))
    for r, rs in enumerate(sub):
        out_ref[rs, :] = _layer_norm(DEEPNORM_ALPHA * hs[r] + mlps[r], g2_ref[...], b2_ref[...])


def _layer_tail(mix_inputs, dilations, x2d, wo, g1, b1, win, wout, g2, b2, *, tm, seq):
    n, d_model = x2d.shape
    d_ff = win.shape[1]
    tiles_per_seq = seq // tm
    n_groups = len(dilations)

    def mix_spec(a, d):
        return pl.BlockSpec((1, d, tm // d, a.shape[-1]),
                            lambda t: (t // tiles_per_seq, 0, t % tiles_per_seq, 0))

    vec = _resident((1, d_model), lambda t: (0, 0))
    kernel = functools.partial(_tail_kernel, dilations=dilations, tm=tm, ff_chunk=min(TAIL_FF_CHUNK, d_ff))
    slabs = mix_inputs[0].shape[-1] // LANES
    n_slots = 2 * n_groups * slabs if any(d > 1 for d in dilations) else 0
    return pl.pallas_call(
        kernel,
        grid=(n // tm,),
        in_specs=[mix_spec(a, dilations[i % n_groups]) for i, a in enumerate(mix_inputs)] + [
            pl.BlockSpec((tm, d_model), lambda t: (t, 0)),
            _resident(wo.shape, lambda t: (0, 0)),
            vec, vec,
            _resident(win.shape, lambda t: (0, 0)),
            _resident(wout.shape, lambda t: (0, 0)),
            vec, vec,
        ],
        out_specs=pl.BlockSpec((tm, d_model), lambda t: (t, 0)),
        out_shape=jax.ShapeDtypeStruct((n, d_model), F32),
        scratch_shapes=[pltpu.VMEM((n_slots, tm, LANES), F32)] if n_slots else [],
        compiler_params=_compiler_params(("parallel",)),
        name=f"layer_tail_{n_groups}",
    )(*mix_inputs, x2d, wo, g1, b1, win, wout, g2, b2)


def kernel(x, moba_w_qkv, moba_w_o, dil_w_qkv, dil_w_o, mlp_w_in, mlp_w_out,
           ln_mix_g, ln_mix_b, ln_mlp_g, ln_mlp_b):
    B, S, D = x.shape
    n_tok = B * S
    tm = min(TAIL_ROW_TILE, S)
    tm_proj = min(PROJ_ROW_TILE, S)
    q_scale = HEAD_DIM ** -0.5 * LOG2E
    cq, sq = _rotary_tables(S, q_scale)
    ck, sk = _rotary_tables(S, 1.0)
    vec = lambda a: a.reshape(1, D).astype(F32)

    w = moba_w_qkv[0]
    hd = w.shape[1] // 3
    wq = _pair_heads(w[:, :hd])
    wk = _pair_heads(w[:, hd:2 * hd])
    wv = w[:, 2 * hd:]
    k, qt, vt, kmean = _qkv_moba(x, wk.astype(BF16), wq.T.astype(BF16), wv.T.astype(BF16),
                                 ck, sk, *_rotary_tables(S, q_scale, transposed=True), tm=tm_proj)
    attn = _moba_attention(qt, k, vt, kmean.reshape(B, S // MOBA_BLOCK, hd),
                           chains=MOBA_CHAINS, pairs=1, lookahead=MOBA_LOOKAHEAD)
    h = _layer_tail([attn.reshape(B, 1, S, hd)], (1,), x.reshape(n_tok, D), moba_w_o[0].astype(BF16),
                    vec(ln_mix_g[0]), vec(ln_mix_b[0]), mlp_w_in[0].astype(BF16), mlp_w_out[0].astype(BF16),
                    vec(ln_mlp_g[0]), vec(ln_mlp_b[0]), tm=tm, seq=S)

    w = dil_w_qkv[0]
    hd = w.shape[1] // 3
    w_all = jnp.concatenate([_pair_heads(w[:, :hd]), _pair_heads(w[:, hd:2 * hd]), w[:, 2 * hd:]], axis=1)
    dilations = tuple(d for _, d in DIL_GROUPS)
    qkv = _qkv_dil(h, w_all.astype(BF16), cq, sq, ck, sk, tm=tm_proj, batch=B, seq=S, dilations=dilations)
    outs, lses = [], []
    for g, (window, dilation) in enumerate(DIL_GROUPS):
        o_g, lse_g = _dilated_group(*qkv[3 * g:3 * g + 3], group=g, window=window, dilation=dilation)
        outs.append(o_g)
        lses.append(lse_g)
    h = _layer_tail(outs + lses, dilations, h, dil_w_o[0].astype(BF16),
                    vec(ln_mix_g[1]), vec(ln_mix_b[1]), mlp_w_in[1].astype(BF16), mlp_w_out[1].astype(BF16),
                    vec(ln_mlp_g[1]), vec(ln_mlp_b[1]), tm=tm, seq=S)
    return h.reshape(B, S, D)
```

```python
import functools
import math

import jax
import jax.numpy as jnp
from jax import lax
from jax.experimental import pallas as pl
from jax.experimental.pallas import tpu as pltpu

HEAD_DIM = 64
HALF_DIM = HEAD_DIM // 2
ROPE_THETA = 10000.0
ROPE_SPLIT = 256
LN_EPS = 1e-5
DEPTH = 2
DEEPNORM_ALPHA = (2.0 * DEPTH) ** 0.25
MOBA_BLOCK = 256
MOBA_TOPK = 3
DIL_GROUPS = ((128, 1), (512, 4), (2048, 16))
DIL_BLOCK = 128

LANES = 128
BF16_SUBLANES = 16
VMEM_LIMIT_BYTES = 56 * 1024 * 1024

PROJ_ROW_TILE = 1024
TAIL_ROW_TILE = 512
TAIL_SUBTILES = 2
TAIL_FF_CHUNK = 1024
MOBA_CHAINS = 4
MOBA_LOOKAHEAD = 5
DIL_ROW_TILE = 8 * DIL_BLOCK
DIL_LOOKAHEAD = 3

LOG2E = math.log2(math.e)

F32 = jnp.float32
BF16 = jnp.bfloat16
NEG_INF = float("-inf")
POS_INF = float("inf")


def _compiler_params(semantics):
    return pltpu.CompilerParams(dimension_semantics=semantics,
                                vmem_limit_bytes=VMEM_LIMIT_BYTES)


def _resident(block_shape, index_map):
    return pl.BlockSpec(block_shape, index_map, pipeline_mode=pl.Buffered(1))


def _pair_heads(w):
    d, hd = w.shape
    w = w.reshape(d, hd // LANES, 2, 2, HALF_DIM)
    return w.transpose(0, 1, 3, 2, 4).reshape(d, hd)


def _rotary_tables(seq, scale, transposed=False):
    inv = 1.0 / (ROPE_THETA ** (jnp.arange(0, HEAD_DIM, 2, dtype=F32) / HEAD_DIM))
    inv = jnp.tile(inv, LANES // HALF_DIM)
    sign = jnp.where(jnp.arange(LANES) < HEAD_DIM, -scale, scale).astype(F32)
    coarse = jnp.arange(0, seq, ROPE_SPLIT, dtype=F32)
    fine = jnp.arange(ROPE_SPLIT, dtype=F32)
    if transposed:
        ang_a = (inv[:, None] * coarse[None, :])[:, :, None]
        ang_b = (inv[:, None] * fine[None, :])[:, None, :]
        sign, shape = sign[:, None], (LANES, seq)
    else:
        ang_a = (coarse[:, None] * inv[None, :])[:, None, :]
        ang_b = (fine[:, None] * inv[None, :])[None, :, :]
        sign, shape = sign[None, :], (seq, LANES)
    cos = (jnp.cos(ang_a) * jnp.cos(ang_b) - jnp.sin(ang_a) * jnp.sin(ang_b)).reshape(shape)
    sin = (jnp.sin(ang_a) * jnp.cos(ang_b) + jnp.cos(ang_a) * jnp.sin(ang_b)).reshape(shape)
    return cos * scale, sin * sign


def _qkv_moba_kernel(x_ref, wk_ref, wqt_ref, wvt_ref, ck_ref, sk_ref, cqt_ref, sqt_ref,
                     k_ref, qt_ref, vt_ref, kmean_ref, *, tm, n_pairs):
    blocks_per_tile = tm // MOBA_BLOCK
    xb = x_ref[0].astype(BF16)

    k = jnp.dot(xb, wk_ref[...], preferred_element_type=F32)
    ck = ck_ref[...]
    sk = sk_ref[...]
    for g in range(n_pairs):
        t = k[:, g * LANES:(g + 1) * LANES]
        kr = t * ck + pltpu.roll(t, HEAD_DIM, axis=1) * sk
        k_ref[0, :, g * LANES:(g + 1) * LANES] = kr.astype(BF16)
        for blk in range(blocks_per_tile):
            mean = jnp.mean(kr[blk * MOBA_BLOCK:(blk + 1) * MOBA_BLOCK], axis=0, keepdims=True)
            kmean_ref[0, 0, blk:blk + 1, g * LANES:(g + 1) * LANES] = mean

    nt_dims = (((1,), (1,)), ((), ()))
    qt = lax.dot_general(wqt_ref[...], xb, nt_dims, preferred_element_type=F32)
    cq = cqt_ref[...]
    sq = sqt_ref[...]
    for g in range(n_pairs):
        t = qt[g * LANES:(g + 1) * LANES]
        swapped = jnp.concatenate([t[HEAD_DIM:], t[:HEAD_DIM]], axis=0)
        qr = (t * cq + swapped * sq).astype(BF16)
        for j in range(blocks_per_tile):
            qt_ref[0, g, j] = qr[:, j * MOBA_BLOCK:(j + 1) * MOBA_BLOCK]

    vt = lax.dot_general(wvt_ref[...], xb, nt_dims, preferred_element_type=F32).astype(BF16)
    for g in range(n_pairs):
        for j in range(blocks_per_tile):
            vt_ref[0, g, j] = vt[g * LANES:(g + 1) * LANES, j * MOBA_BLOCK:(j + 1) * MOBA_BLOCK]


def _qkv_moba(x, wk, wqt, wvt, ck, sk, cqt, sqt, *, tm):
    B, S, D = x.shape
    hd = wk.shape[1]
    n_pairs = hd // LANES
    nb = S // MOBA_BLOCK
    kernel = functools.partial(_qkv_moba_kernel, tm=tm, n_pairs=n_pairs)
    bpt = tm // MOBA_BLOCK
    return pl.pallas_call(
        kernel,
        grid=(B, S // tm),
        in_specs=[
            pl.BlockSpec((1, tm, D), lambda b, t: (b, t, 0)),
            _resident((D, hd), lambda b, t: (0, 0)),
            _resident((hd, D), lambda b, t: (0, 0)),
            _resident((hd, D), lambda b, t: (0, 0)),
            pl.BlockSpec((tm, LANES), lambda b, t: (t, 0)),
            pl.BlockSpec((tm, LANES), lambda b, t: (t, 0)),
            pl.BlockSpec((LANES, tm), lambda b, t: (0, t)),
            pl.BlockSpec((LANES, tm), lambda b, t: (0, t)),
        ],
        out_specs=[
            pl.BlockSpec((1, tm, hd), lambda b, t: (b, t, 0)),
            pl.BlockSpec((1, n_pairs, bpt, LANES, MOBA_BLOCK), lambda b, t: (b, 0, t, 0, 0)),
            pl.BlockSpec((1, n_pairs, bpt, LANES, MOBA_BLOCK), lambda b, t: (b, 0, t, 0, 0)),
            pl.BlockSpec((1, 1, bpt, hd), lambda b, t: (b, t, 0, 0)),
        ],
        out_shape=[
            jax.ShapeDtypeStruct((B, S, hd), BF16),
            jax.ShapeDtypeStruct((B, n_pairs, nb, LANES, MOBA_BLOCK), BF16),
            jax.ShapeDtypeStruct((B, n_pairs, nb, LANES, MOBA_BLOCK), BF16),
            jax.ShapeDtypeStruct((B, S // tm, bpt, hd), F32),
        ],
        compiler_params=_compiler_params(("parallel", "parallel")),
        name="qkv_moba",
    )(x, wk, wqt, wvt, ck, sk, cqt, sqt)


def _top3_blocks(gate, n_past):
    nb = gate.shape[0]
    row_i = lax.broadcasted_iota(jnp.int32, gate.shape, 0)
    row = row_i.astype(F32)
    g = jnp.where(row_i < n_past, gate, NEG_INF)
    sel = jnp.zeros(gate.shape, F32)
    for _ in range(MOBA_TOPK):
        mx = jnp.max(g, axis=0, keepdims=True)
        is_mx = jnp.logical_and(g == mx, g > NEG_INF)
        first = jnp.min(jnp.where(is_mx, row, nb), axis=0, keepdims=True)
        pick = row == first
        sel = jnp.where(pick, 1.0, sel)
        g = jnp.where(pick, NEG_INF, g)
    return sel


def _moba_kernel(qt_ref, k_ref, vt_ref, kmean_ref, o_ref, sel_ref, acc_ref, merged_ref, *,
                 nb, chains, pairs, lookahead):
    blk = MOBA_BLOCK
    n_heads = 2 * pairs
    n_items = nb + 1
    kmean = kmean_ref[0].astype(BF16)
    zeros_half = jnp.zeros((HALF_DIM, blk), BF16)
    ones_rows = jnp.ones((BF16_SUBLANES, blk), BF16)
    key_idx = lax.broadcasted_iota(jnp.int32, (blk, blk), 0)
    qry_idx = lax.broadcasted_iota(jnp.int32, (blk, blk), 1)
    causal = key_idx <= qry_idx

    def q_head(tile, h):
        qt = qt_ref[0, h // 2, tile]
        lo = (h % 2) * HALF_DIM
        rows = [zeros_half] * (4 * pairs)
        rows[4 * (h // 2) + h % 2] = qt[lo:lo + HALF_DIM]
        rows[4 * (h // 2) + 2 + h % 2] = qt[HEAD_DIM + lo:HEAD_DIM + lo + HALF_DIM]
        return jnp.concatenate(rows, axis=0)

    def values_aug(j, h):
        vt_pair = vt_ref[0, h // 2, j]
        return jnp.concatenate([vt_pair[(h % 2) * HEAD_DIM:(h % 2 + 1) * HEAD_DIM], ones_rows], axis=0)

    def pair_tiles(pair):
        return pair, nb - 1 - pair

    def choose(pair):
        tiles = pair_tiles(pair)
        return {(x, h): _top3_blocks(jnp.dot(kmean, q_head(tiles[x], h), preferred_element_type=F32), tiles[x])
                for x in range(2) for h in range(n_heads)}

    def write_merged(pair):
        tiles = pair_tiles(pair)
        for x in range(2):
            outs = []
            for h in range(n_heads):
                acc = merged_ref[x, h]
                outs.append(acc[:HEAD_DIM] * (1.0 / acc[HEAD_DIM:HEAD_DIM + 1]))
            o_all = jnp.concatenate(outs, axis=0)
            o_ref[0, pl.ds(pl.multiple_of(tiles[x] * blk, blk), blk), :] = o_all.T.astype(BF16)

    n_pairs = nb // 2
    acc_ref[...] = jnp.ones(acc_ref.shape, F32)
    merged_ref[...] = jnp.ones(merged_ref.shape, F32)
    for key, sel in choose(0).items():
        sel_ref[(0,) + key] = sel

    def tile_pair(a, carry):
        slot = a % 2
        tiles = pair_tiles(a)

        def item(t):
            if t < 2:
                return t == 0, t, tiles[t], tiles[t], True
            past = t - 2
            first = past < a
            slot_x = jnp.where(first, 0, 1)
            return first, slot_x, jnp.where(first, a, nb - 1 - a), jnp.where(first, past, past - a), False

        def chain_scores(t, h):
            _, _, tile, block, _ = item(t)
            k_j = k_ref[0, pl.ds(pl.multiple_of(block * blk, blk), blk), :]
            return jnp.dot(k_j, q_head(tile, h), preferred_element_type=F32)

        neg = jnp.full((1, blk), NEG_INF, F32)
        m_run = {(x, h, u): neg for x in range(2) for h in range(n_heads) for u in range(chains)}

        n_chains = n_items * n_heads
        next_sel = None
        scores = {j: chain_scores(j // n_heads, j % n_heads) for j in range(lookahead)}
        for j in range(n_chains):
            if j == n_chains // 8:
                next_sel = choose(jnp.minimum(a + 1, n_pairs - 1))
            if j == n_chains // 4:
                write_merged(jnp.maximum(a - 1, 0))
            t, h = j // n_heads, j % n_heads
            u = t % chains
            first, x, _, block, own = item(t)
            s = scores.pop(j)
            if j + lookahead < n_chains:
                scores[j + lookahead] = chain_scores((j + lookahead) // n_heads, (j + lookahead) % n_heads)
            if own:
                s = jnp.where(causal, s, NEG_INF)
                m_old = m_run[(x, h, u)]
                m_new = jnp.maximum(m_old, jnp.max(s, axis=0, keepdims=True))
                m_run[(x, h, u)] = m_new
                shift = m_new
            else:
                chosen = sel_ref[slot, x, h, pl.ds(block, 1), :] > 0.0
                m_old = jnp.where(first, m_run[(0, h, u)], m_run[(1, h, u)])
                m_new = jnp.maximum(m_old, jnp.where(chosen, jnp.max(s, axis=0, keepdims=True), NEG_INF))
                m_run[(0, h, u)] = jnp.where(first, m_new, m_run[(0, h, u)])
                m_run[(1, h, u)] = jnp.where(first, m_run[(1, h, u)], m_new)
                m_new = jnp.where(m_new == NEG_INF, 0.0, m_new)
                shift = jnp.where(chosen, m_new, POS_INF)
            p = jnp.exp2(s - shift).astype(BF16)
            pv = jnp.dot(values_aug(block, h), p, preferred_element_type=F32)
            acc_ref[x, h, u] = acc_ref[x, h, u] * jnp.exp2(m_old - m_new) + pv

        for x in range(2):
            for h in range(n_heads):
                ms = [m_run[(x, h, u)] for u in range(chains)]
                m_all = functools.reduce(jnp.maximum, ms)
                merged_ref[x, h] = functools.reduce(
                    jnp.add, [acc_ref[x, h, u] * jnp.exp2(ms[u] - m_all) for u in range(chains)])
        for key, sel in next_sel.items():
            sel_ref[(1 - slot,) + key] = sel
        return carry

    lax.fori_loop(0, n_pairs, tile_pair, 0)
    write_merged(n_pairs - 1)


def _moba_attention(qt, k, vt, kmean, *, chains, pairs, lookahead):
    B, n_pairs, nb, _, blk = qt.shape
    S = k.shape[1]
    hd = k.shape[2]
    assert nb % 2 == 0 and n_pairs % pairs == 0
    n_heads = 2 * pairs
    width = pairs * LANES
    kernel = functools.partial(_moba_kernel, nb=nb, chains=chains, pairs=pairs, lookahead=lookahead)
    return pl.pallas_call(
        kernel,
        grid=(B, n_pairs // pairs),
        in_specs=[
            pl.BlockSpec((1, pairs, nb, LANES, blk), lambda b, p: (b, p, 0, 0, 0)),
            pl.BlockSpec((1, S, width), lambda b, p: (b, 0, p)),
            pl.BlockSpec((1, pairs, nb, LANES, blk), lambda b, p: (b, p, 0, 0, 0)),
            pl.BlockSpec((1, nb, width), lambda b, p: (b, 0, p)),
        ],
        out_specs=pl.BlockSpec((1, S, width), lambda b, p: (b, 0, p)),
        out_shape=jax.ShapeDtypeStruct((B, S, hd), BF16),
        scratch_shapes=[
            pltpu.VMEM((2, 2, n_heads, nb, blk), F32),
            pltpu.VMEM((2, n_heads, chains, HEAD_DIM + BF16_SUBLANES, blk), F32),
            pltpu.VMEM((2, n_heads, HEAD_DIM + BF16_SUBLANES, blk), F32),
        ],
        compiler_params=_compiler_params(("parallel", "parallel")),
        name="moba_attention",
    )(qt, k, vt, kmean)


def _qkv_dil_kernel(x_ref, w_ref, cq_ref, sq_ref, ck_ref, sk_ref, *refs, hd, dilations, tm):
    out_refs, scr_ref = refs[:-1], refs[-1]
    gw = hd // len(dilations)
    xb = x_ref[...].astype(BF16)
    qkv = jnp.dot(xb, w_ref[...], preferred_element_type=F32)
    slot = 0
    for g, d in enumerate(dilations):
        for kind in range(3):
            ref = out_refs[3 * g + kind]
            for slab in range(gw // LANES):
                col = kind * hd + g * gw + slab * LANES
                t = qkv[:, col:col + LANES]
                if kind == 0:
                    t = t * cq_ref[...] + pltpu.roll(t, HEAD_DIM, axis=1) * sq_ref[...]
                elif kind == 1:
                    t = t * ck_ref[...] + pltpu.roll(t, HEAD_DIM, axis=1) * sk_ref[...]
                lanes = slice(slab * LANES, (slab + 1) * LANES)
                if d == 1:
                    ref[0, 0, :, lanes] = t.astype(BF16)
                else:
                    scr_ref[slot] = t
                    for r in range(d):
                        ref[0, r, :, lanes] = scr_ref[slot, pl.ds(r, tm // d, stride=d), :].astype(BF16)
                    slot += 1


def _qkv_dil(x2d, w, cq, sq, ck, sk, *, tm, batch, seq, dilations):
    n, d_model = x2d.shape
    hd = w.shape[1] // 3
    gw = hd // len(dilations)
    tiles_per_seq = seq // tm
    kernel = functools.partial(_qkv_dil_kernel, hd=hd, dilations=dilations, tm=tm)
    tab = pl.BlockSpec((tm, LANES), lambda t: (t % tiles_per_seq, 0))
    out_specs, out_shape = [], []
    for d in dilations:
        for _ in range(3):
            out_specs.append(pl.BlockSpec((1, d, tm // d, gw),
                                          lambda t: (t // tiles_per_seq, 0, t % tiles_per_seq, 0)))
            out_shape.append(jax.ShapeDtypeStruct((batch, d, seq // d, gw), BF16))
    n_slots = sum(3 * (gw // LANES) for d in dilations if d > 1)
    return pl.pallas_call(
        kernel,
        grid=(n // tm,),
        in_specs=[pl.BlockSpec((tm, d_model), lambda t: (t, 0)),
                  _resident((d_model, 3 * hd), lambda t: (0, 0)),
                  tab, tab, tab, tab],
        out_specs=out_specs,
        out_shape=out_shape,
        scratch_shapes=[pltpu.VMEM((max(n_slots, 1), tm, LANES), F32)],
        compiler_params=_compiler_params(("parallel",)),
        name="qkv_dilated",
    )(x2d, w, cq, sq, ck, sk)


def _dilated_kernel(q_ref, kc_ref, kh_ref, vc_ref, vh_ref, o_ref, lse_ref, *, tl, span):
    wb = DIL_BLOCK
    n_blk = pl.program_id(2)
    n_res = q_ref.shape[0]
    q = [q_ref[c] for c in range(n_res)]
    k_all = [jnp.concatenate([kh_ref[c], kc_ref[c]], axis=0) for c in range(n_res)]
    v_all = [jnp.concatenate([vh_ref[c], vc_ref[c]], axis=0) for c in range(n_res)]
    lane = lax.broadcasted_iota(jnp.int32, (wb, LANES), 1)
    first_head_lanes = (lane % HEAD_DIM) < HALF_DIM
    first_head_out = lane < HEAD_DIM
    qq = lax.broadcasted_iota(jnp.int32, (wb, 2 * wb), 0)
    kk = lax.broadcasted_iota(jnp.int32, (wb, 2 * wb), 1)
    dist = qq + wb - kk
    band = jnp.logical_and(dist >= 0, dist <= span)
    nt_dims = (((1,), (1,)), ((), ()))
    units = [(c, pair, sub, h) for c in range(n_res) for pair in range(q[0].shape[1] // LANES)
             for sub in range(tl // wb) for h in range(2)]

    def unit_scores(c, pair, sub, h):
        lo, hi = pair * LANES, (pair + 1) * LANES
        qs = q[c][sub * wb:(sub + 1) * wb, lo:hi]
        ks = k_all[c][sub * wb:sub * wb + 2 * wb, lo:hi]
        keep = first_head_lanes if h == 0 else jnp.logical_not(first_head_lanes)
        qm = jnp.where(keep, qs, jnp.zeros_like(qs))
        return lax.dot_general(qm, ks, nt_dims, preferred_element_type=F32)

    ahead = DIL_LOOKAHEAD
    scores = {n: unit_scores(*units[n]) for n in range(ahead)}
    outs, lses = {}, {}
    for n, (c, pair, sub, h) in enumerate(units):
        lo, hi = pair * LANES, (pair + 1) * LANES
        s = scores.pop(n)
        if n + ahead < len(units):
            scores[n + ahead] = unit_scores(*units[n + ahead])
        mask = jnp.logical_and(band, jnp.logical_or(kk >= wb, n_blk > 0)) if sub == 0 else band
        s = jnp.where(mask, s, NEG_INF)
        m = jnp.max(s, axis=1, keepdims=True)
        p = jnp.exp2(s - m)
        l = jnp.sum(p, axis=1, keepdims=True)
        vs = v_all[c][sub * wb:sub * wb + 2 * wb, lo:hi]
        pv = jnp.dot(p.astype(BF16), vs, preferred_element_type=F32)
        outs[h] = pv * (1.0 / l)
        lses[h] = m + jnp.log2(l)
        if h == 1:
            rows = slice(sub * wb, (sub + 1) * wb)
            o_ref[c, rows, lo:hi] = jnp.where(first_head_out, outs[0], outs[1]).astype(BF16)
            lse_ref[c, rows, lo:hi] = jnp.where(
                first_head_out, jnp.broadcast_to(lses[0], (wb, LANES)), jnp.broadcast_to(lses[1], (wb, LANES)))


def _dilated_group(q, k, v, *, group, window, dilation):
    batch, _, L, gw = q.shape
    span = window // dilation
    assert span <= DIL_BLOCK and L % DIL_BLOCK == 0
    tl = min(DIL_ROW_TILE, L)
    halo_per_tile = tl // DIL_BLOCK
    n_res = max(1, min(dilation, DIL_ROW_TILE // tl))
    cur = pl.BlockSpec((None, n_res, tl, gw), lambda b, r, n: (b, r, n, 0))
    halo = pl.BlockSpec((None, n_res, DIL_BLOCK, gw),
                        lambda b, r, n: (b, r, jnp.maximum(n * halo_per_tile - 1, 0), 0))
    kernel = functools.partial(_dilated_kernel, tl=tl, span=span)
    return pl.pallas_call(
        kernel,
        grid=(batch, dilation // n_res, L // tl),
        in_specs=[cur, cur, halo, cur, halo],
        out_specs=[cur, cur],
        out_shape=[jax.ShapeDtypeStruct(q.shape, BF16), jax.ShapeDtypeStruct(q.shape, F32)],
        compiler_params=_compiler_params(("parallel", "parallel", "arbitrary")),
        name=f"dilated_attention_g{group}",
    )(q, k, k, v, v)


def _layer_norm(y, g, b):
    mu = jnp.mean(y, axis=-1, keepdims=True)
    yc = y - mu
    var = jnp.mean(yc * yc, axis=-1, keepdims=True)
    return yc * lax.rsqrt(var + LN_EPS) * g + b


def _position_order(ref, d, scr_ref, slot, tm):
    if d == 1:
        return ref[0, 0]
    n_slabs = ref.shape[-1] // LANES
    for slab in range(n_slabs):
        for r in range(d):
            scr_ref[slot + slab, pl.ds(r, tm // d, stride=d), :] = (
                ref[0, r, :, slab * LANES:(slab + 1) * LANES].astype(F32))
    return jnp.concatenate([scr_ref[slot + slab] for slab in range(n_slabs)], axis=1)


def _tail_kernel(*refs, dilations, tm, ff_chunk):
    n_groups = len(dilations)
    n_mix = 2 * n_groups if n_groups > 1 else 1
    mix_refs = refs[:n_mix]
    (x_ref, wo_ref, g1_ref, b1_ref, win_ref, wout_ref, g2_ref, b2_ref, out_ref) = refs[n_mix:n_mix + 9]
    scr_ref = refs[n_mix + 9] if len(refs) > n_mix + 9 else None
    if n_groups == 1:
        o = _position_order(mix_refs[0], dilations[0], scr_ref, 0, tm)
    else:
        slabs = mix_refs[0].shape[-1] // LANES
        outs = [_position_order(mix_refs[g], d, scr_ref, 2 * g * slabs, tm).astype(F32)
                for g, d in enumerate(dilations)]
        lses = [_position_order(mix_refs[n_groups + g], d, scr_ref, (2 * g + 1) * slabs, tm)
                for g, d in enumerate(dilations)]
        top = functools.reduce(jnp.maximum, lses)
        ws = [jnp.exp2(l - top) for l in lses]
        inv = 1.0 / functools.reduce(jnp.add, ws)
        o = jnp.concatenate([(outs[g] * (ws[g] * inv)).astype(BF16) for g in range(n_groups)], axis=1)
    d_ff = win_ref.shape[1]
    rows = tm // TAIL_SUBTILES

    def mlp(hb):
        acc = jnp.zeros(hb.shape, F32)
        for c in range(d_ff // ff_chunk):
            lo, hi = c * ff_chunk, (c + 1) * ff_chunk
            hid = jnp.maximum(jnp.dot(hb, win_ref[:, lo:hi], preferred_element_type=F32), 0.0)
            acc = acc + jnp.dot((hid * hid).astype(BF16), wout_ref[lo:hi, :], preferred_element_type=F32)
        return acc

    sub = [slice(r * rows, (r + 1) * rows) for r in range(TAIL_SUBTILES)]
    mixes = [jnp.dot(o[rs], wo_ref[...], preferred_element_type=F32) for rs in sub]
    hs, mlps = [], []
    for r, rs in enumerate(sub):
        hs.append(_layer_norm(DEEPNORM_ALPHA * x_ref[rs, :] + mixes[r], g1_ref[...], b1_ref[...]))
        mlps.append(mlp(hs[r].astype(BF16)))
    for r, rs in enumerate(sub):
        out_ref[rs, :] = _layer_norm(DEEPNORM_ALPHA * hs[r] + mlps[r], g2_ref[...], b2_ref[...])


def _layer_tail(mix_inputs, dilations, x2d, wo, g1, b1, win, wout, g2, b2, *, tm, seq):
    n, d_model = x2d.shape
    d_ff = win.shape[1]
    tiles_per_seq = seq // tm
    n_groups = len(dilations)

    def mix_spec(a, d):
        return pl.BlockSpec((1, d, tm // d, a.shape[-1]),
                            lambda t: (t // tiles_per_seq, 0, t % tiles_per_seq, 0))

    vec = _resident((1, d_model), lambda t: (0, 0))
    kernel = functools.partial(_tail_kernel, dilations=dilations, tm=tm, ff_chunk=min(TAIL_FF_CHUNK, d_ff))
    slabs = mix_inputs[0].shape[-1] // LANES
    n_slots = 2 * n_groups * slabs if any(d > 1 for d in dilations) else 0
    return pl.pallas_call(
        kernel,
        grid=(n // tm,),
        in_specs=[mix_spec(a, dilations[i % n_groups]) for i, a in enumerate(mix_inputs)] + [
            pl.BlockSpec((tm, d_model), lambda t: (t, 0)),
            _resident(wo.shape, lambda t: (0, 0)),
            vec, vec,
            _resident(win.shape, lambda t: (0, 0)),
            _resident(wout.shape, lambda t: (0, 0)),
            vec, vec,
        ],
        out_specs=pl.BlockSpec((tm, d_model), lambda t: (t, 0)),
        out_shape=jax.ShapeDtypeStruct((n, d_model), F32),
        scratch_shapes=[pltpu.VMEM((n_slots, tm, LANES), F32)] if n_slots else [],
        compiler_params=_compiler_params(("parallel",)),
        name=f"layer_tail_{n_groups}",
    )(*mix_inputs, x2d, wo, g1, b1, win, wout, g2, b2)


def kernel(x, moba_w_qkv, moba_w_o, dil_w_qkv, dil_w_o, mlp_w_in, mlp_w_out,
           ln_mix_g, ln_mix_b, ln_mlp_g, ln_mlp_b):
    B, S, D = x.shape
    n_tok = B * S
    tm = min(TAIL_ROW_TILE, S)
    tm_proj = min(PROJ_ROW_TILE, S)
    q_scale = HEAD_DIM ** -0.5 * LOG2E
    cq, sq = _rotary_tables(S, q_scale)
    ck, sk = _rotary_tables(S, 1.0)
    vec = lambda a: a.reshape(1, D).astype(F32)

    w = moba_w_qkv[0]
    hd = w.shape[1] // 3
    wq = _pair_heads(w[:, :hd])
    wk = _pair_heads(w[:, hd:2 * hd])
    wv = w[:, 2 * hd:]
    k, qt, vt, kmean = _qkv_moba(x, wk.astype(BF16), wq.T.astype(BF16), wv.T.astype(BF16),
                                 ck, sk, *_rotary_tables(S, q_scale, transposed=True), tm=tm_proj)
    attn = _moba_attention(qt, k, vt, kmean.reshape(B, S // MOBA_BLOCK, hd),
                           chains=MOBA_CHAINS, pairs=1, lookahead=MOBA_LOOKAHEAD)
    h = _layer_tail([attn.reshape(B, 1, S, hd)], (1,), x.reshape(n_tok, D), moba_w_o[0].astype(BF16),
                    vec(ln_mix_g[0]), vec(ln_mix_b[0]), mlp_w_in[0].astype(BF16), mlp_w_out[0].astype(BF16),
                    vec(ln_mlp_g[0]), vec(ln_mlp_b[0]), tm=tm, seq=S)

    w = dil_w_qkv[0]
    hd = w.shape[1] // 3
    w_all = jnp.concatenate([_pair_heads(w[:, :hd]), _pair_heads(w[:, hd:2 * hd]), w[:, 2 * hd:]], axis=1)
    dilations = tuple(d for _, d in DIL_GROUPS)
    qkv = _qkv_dil(h, w_all.astype(BF16), cq, sq, ck, sk, tm=tm_proj, batch=B, seq=S, dilations=dilations)
    outs, lses = [], []
    for g, (window, dilation) in enumerate(DIL_GROUPS):
        o_g, lse_g = _dilated_group(*qkv[3 * g:3 * g + 3], group=g, window=window, dilation=dilation)
        outs.append(o_g)
        lses.append(lse_g)
    h = _layer_tail(outs + lses, dilations, h, dil_w_o[0].astype(BF16),
                    vec(ln_mix_g[1]), vec(ln_mix_b[1]), mlp_w_in[1].astype(BF16), mlp_w_out[1].astype(BF16),
                    vec(ln_mlp_g[1]), vec(ln_mlp_b[1]), tm=tm, seq=S)
    return h.reshape(B, S, D)
```

```python
import functools
import math

import jax
import jax.numpy as jnp
from jax import lax
from jax.experimental import pallas as pl
from jax.experimental.pallas import tpu as pltpu

HEAD_DIM = 64
HALF_DIM = HEAD_DIM // 2
ROPE_THETA = 10000.0
ROPE_SPLIT = 256
LN_EPS = 1e-5
DEPTH = 2
DEEPNORM_ALPHA = (2.0 * DEPTH) ** 0.25
MOBA_BLOCK = 256
MOBA_TOPK = 3
DIL_GROUPS = ((128, 1), (512, 4), (2048, 16))
DIL_BLOCK = 128

LANES = 128
BF16_SUBLANES = 16
VMEM_LIMIT_BYTES = 56 * 1024 * 1024

PROJ_ROW_TILE = 1024
TAIL_ROW_TILE = 512
TAIL_SUBTILES = 2
TAIL_FF_CHUNK = 1024
MOBA_CHAINS = 4
MOBA_LOOKAHEAD = 5
DIL_ROW_TILE = 8 * DIL_BLOCK
DIL_LOOKAHEAD = 3

LOG2E = math.log2(math.e)

F32 = jnp.float32
BF16 = jnp.bfloat16
NEG_INF = float("-inf")
POS_INF = float("inf")


def _compiler_params(semantics, allow_input_fusion=None):
    return pltpu.CompilerParams(dimension_semantics=semantics, allow_input_fusion=allow_input_fusion,
                                vmem_limit_bytes=VMEM_LIMIT_BYTES)


def _resident(block_shape, index_map):
    return pl.BlockSpec(block_shape, index_map, pipeline_mode=pl.Buffered(1))


def _pair_heads(w):
    d, hd = w.shape
    w = w.reshape(d, hd // LANES, 2, 2, HALF_DIM)
    return w.transpose(0, 1, 3, 2, 4).reshape(d, hd)


def _rotary_tables(seq, scale, transposed=False):
    inv = 1.0 / (ROPE_THETA ** (jnp.arange(0, HEAD_DIM, 2, dtype=F32) / HEAD_DIM))
    inv = jnp.tile(inv, LANES // HALF_DIM)
    sign = jnp.where(jnp.arange(LANES) < HEAD_DIM, -scale, scale).astype(F32)
    coarse = jnp.arange(0, seq, ROPE_SPLIT, dtype=F32)
    fine = jnp.arange(ROPE_SPLIT, dtype=F32)
    if transposed:
        ang_a = (inv[:, None] * coarse[None, :])[:, :, None]
        ang_b = (inv[:, None] * fine[None, :])[:, None, :]
        sign, shape = sign[:, None], (LANES, seq)
    else:
        ang_a = (coarse[:, None] * inv[None, :])[:, None, :]
        ang_b = (fine[:, None] * inv[None, :])[None, :, :]
        sign, shape = sign[None, :], (seq, LANES)
    cos = (jnp.cos(ang_a) * jnp.cos(ang_b) - jnp.sin(ang_a) * jnp.sin(ang_b)).reshape(shape)
    sin = (jnp.sin(ang_a) * jnp.cos(ang_b) + jnp.cos(ang_a) * jnp.sin(ang_b)).reshape(shape)
    return cos * scale, sin * sign


def _qkv_moba_kernel(x_ref, wk_ref, wqt_ref, wvt_ref, ck_ref, sk_ref, cqt_ref, sqt_ref,
                     k_ref, qt_ref, vt_ref, kmean_ref, *, tm, n_pairs):
    blocks_per_tile = tm // MOBA_BLOCK
    xb = x_ref[0].astype(BF16)

    k = jnp.dot(xb, wk_ref[...], preferred_element_type=F32)
    ck = ck_ref[...]
    sk = sk_ref[...]
    for g in range(n_pairs):
        t = k[:, g * LANES:(g + 1) * LANES]
        kr = t * ck + pltpu.roll(t, HEAD_DIM, axis=1) * sk
        k_ref[0, :, g * LANES:(g + 1) * LANES] = kr.astype(BF16)
        for blk in range(blocks_per_tile):
            mean = jnp.mean(kr[blk * MOBA_BLOCK:(blk + 1) * MOBA_BLOCK], axis=0, keepdims=True)
            kmean_ref[0, 0, blk:blk + 1, g * LANES:(g + 1) * LANES] = mean

    nt_dims = (((1,), (1,)), ((), ()))
    qt = lax.dot_general(wqt_ref[...], xb, nt_dims, preferred_element_type=F32)
    cq = cqt_ref[...]
    sq = sqt_ref[...]
    for g in range(n_pairs):
        t = qt[g * LANES:(g + 1) * LANES]
        swapped = jnp.concatenate([t[HEAD_DIM:], t[:HEAD_DIM]], axis=0)
        qr = (t * cq + swapped * sq).astype(BF16)
        for j in range(blocks_per_tile):
            qt_ref[0, g, j] = qr[:, j * MOBA_BLOCK:(j + 1) * MOBA_BLOCK]

    vt = lax.dot_general(wvt_ref[...], xb, nt_dims, preferred_element_type=F32).astype(BF16)
    for g in range(n_pairs):
        for j in range(blocks_per_tile):
            vt_ref[0, g, j] = vt[g * LANES:(g + 1) * LANES, j * MOBA_BLOCK:(j + 1) * MOBA_BLOCK]


def _qkv_moba(x, wk, wqt, wvt, ck, sk, cqt, sqt, *, tm):
    B, S, D = x.shape
    hd = wk.shape[1]
    n_pairs = hd // LANES
    nb = S // MOBA_BLOCK
    kernel = functools.partial(_qkv_moba_kernel, tm=tm, n_pairs=n_pairs)
    bpt = tm // MOBA_BLOCK
    return pl.pallas_call(
        kernel,
        grid=(B, S // tm),
        in_specs=[
            pl.BlockSpec((1, tm, D), lambda b, t: (b, t, 0)),
            _resident((D, hd), lambda b, t: (0, 0)),
            _resident((hd, D), lambda b, t: (0, 0)),
            _resident((hd, D), lambda b, t: (0, 0)),
            pl.BlockSpec((tm, LANES), lambda b, t: (t, 0)),
            pl.BlockSpec((tm, LANES), lambda b, t: (t, 0)),
            pl.BlockSpec((LANES, tm), lambda b, t: (0, t)),
            pl.BlockSpec((LANES, tm), lambda b, t: (0, t)),
        ],
        out_specs=[
            pl.BlockSpec((1, tm, hd), lambda b, t: (b, t, 0)),
            pl.BlockSpec((1, n_pairs, bpt, LANES, MOBA_BLOCK), lambda b, t: (b, 0, t, 0, 0)),
            pl.BlockSpec((1, n_pairs, bpt, LANES, MOBA_BLOCK), lambda b, t: (b, 0, t, 0, 0)),
            pl.BlockSpec((1, 1, bpt, hd), lambda b, t: (b, t, 0, 0)),
        ],
        out_shape=[
            jax.ShapeDtypeStruct((B, S, hd), BF16),
            jax.ShapeDtypeStruct((B, n_pairs, nb, LANES, MOBA_BLOCK), BF16),
            jax.ShapeDtypeStruct((B, n_pairs, nb, LANES, MOBA_BLOCK), BF16),
            jax.ShapeDtypeStruct((B, S // tm, bpt, hd), F32),
        ],
        compiler_params=_compiler_params(("parallel", "parallel")),
        name="qkv_moba",
    )(x, wk, wqt, wvt, ck, sk, cqt, sqt)


def _top3_blocks(gate, n_past):
    nb = gate.shape[0]
    row_i = lax.broadcasted_iota(jnp.int32, gate.shape, 0)
    row = row_i.astype(F32)
    g = jnp.where(row_i < n_past, gate, NEG_INF)
    sel = jnp.zeros(gate.shape, F32)
    for _ in range(MOBA_TOPK):
        mx = jnp.max(g, axis=0, keepdims=True)
        is_mx = jnp.logical_and(g == mx, g > NEG_INF)
        first = jnp.min(jnp.where(is_mx, row, nb), axis=0, keepdims=True)
        pick = row == first
        sel = jnp.where(pick, 1.0, sel)
        g = jnp.where(pick, NEG_INF, g)
    return sel


def _moba_kernel(qt_ref, k_ref, vt_ref, kmean_ref, o_ref, sel_ref, acc_ref, merged_ref, *,
                 nb, chains, pairs, lookahead):
    blk = MOBA_BLOCK
    n_heads = 2 * pairs
    n_items = nb + 1
    kmean = kmean_ref[0].astype(BF16)
    zeros_half = jnp.zeros((HALF_DIM, blk), BF16)
    ones_rows = jnp.ones((BF16_SUBLANES, blk), BF16)
    key_idx = lax.broadcasted_iota(jnp.int32, (blk, blk), 0)
    qry_idx = lax.broadcasted_iota(jnp.int32, (blk, blk), 1)
    causal = key_idx <= qry_idx

    def q_head(tile, h):
        qt = qt_ref[0, h // 2, tile]
        lo = (h % 2) * HALF_DIM
        rows = [zeros_half] * (4 * pairs)
        rows[4 * (h // 2) + h % 2] = qt[lo:lo + HALF_DIM]
        rows[4 * (h // 2) + 2 + h % 2] = qt[HEAD_DIM + lo:HEAD_DIM + lo + HALF_DIM]
        return jnp.concatenate(rows, axis=0)

    def values_aug(j, h):
        vt_pair = vt_ref[0, h // 2, j]
        return jnp.concatenate([vt_pair[(h % 2) * HEAD_DIM:(h % 2 + 1) * HEAD_DIM], ones_rows], axis=0)

    def pair_tiles(pair):
        return pair, nb - 1 - pair

    def choose(pair):
        tiles = pair_tiles(pair)
        return {(x, h): _top3_blocks(jnp.dot(kmean, q_head(tiles[x], h), preferred_element_type=F32), tiles[x])
                for x in range(2) for h in range(n_heads)}

    def write_merged(pair):
        tiles = pair_tiles(pair)
        for x in range(2):
            outs = []
            for h in range(n_heads):
                acc = merged_ref[x, h]
                outs.append(acc[:HEAD_DIM] * (1.0 / acc[HEAD_DIM:HEAD_DIM + 1]))
            o_all = jnp.concatenate(outs, axis=0)
            o_ref[0, pl.ds(pl.multiple_of(tiles[x] * blk, blk), blk), :] = o_all.T.astype(BF16)

    n_pairs = nb // 2
    acc_ref[...] = jnp.ones(acc_ref.shape, F32)
    merged_ref[...] = jnp.ones(merged_ref.shape, F32)
    for key, sel in choose(0).items():
        sel_ref[(0,) + key] = sel

    def tile_pair(a, carry):
        slot = a % 2
        tiles = pair_tiles(a)

        def item(t):
            if t < 2:
                return t == 0, t, tiles[t], tiles[t], True
            past = t - 2
            first = past < a
            slot_x = jnp.where(first, 0, 1)
            return first, slot_x, jnp.where(first, a, nb - 1 - a), jnp.where(first, past, past - a), False

        def chain_scores(t, h):
            _, _, tile, block, _ = item(t)
            k_j = k_ref[0, pl.ds(pl.multiple_of(block * blk, blk), blk), :]
            return jnp.dot(k_j, q_head(tile, h), preferred_element_type=F32)

        neg = jnp.full((1, blk), NEG_INF, F32)
        m_run = {(x, h, u): neg for x in range(2) for h in range(n_heads) for u in range(chains)}

        n_chains = n_items * n_heads
        next_sel = None
        scores = {j: chain_scores(j // n_heads, j % n_heads) for j in range(lookahead)}
        for j in range(n_chains):
            if j == n_chains // 8:
                next_sel = choose(jnp.minimum(a + 1, n_pairs - 1))
            if j == n_chains // 4:
                write_merged(jnp.maximum(a - 1, 0))
            t, h = j // n_heads, j % n_heads
            u = t % chains
            first, x, _, block, own = item(t)
            s = scores.pop(j)
            if j + lookahead < n_chains:
                scores[j + lookahead] = chain_scores((j + lookahead) // n_heads, (j + lookahead) % n_heads)
            if own:
                s = jnp.where(causal, s, NEG_INF)
                m_old = m_run[(x, h, u)]
                m_new = jnp.maximum(m_old, jnp.max(s, axis=0, keepdims=True))
                m_run[(x, h, u)] = m_new
                shift = m_new
            else:
                chosen = sel_ref[slot, x, h, pl.ds(block, 1), :] > 0.0
                m_old = jnp.where(first, m_run[(0, h, u)], m_run[(1, h, u)])
                m_new = jnp.maximum(m_old, jnp.where(chosen, jnp.max(s, axis=0, keepdims=True), NEG_INF))
                m_run[(0, h, u)] = jnp.where(first, m_new, m_run[(0, h, u)])
                m_run[(1, h, u)] = jnp.where(first, m_run[(1, h, u)], m_new)
                m_new = jnp.where(m_new == NEG_INF, 0.0, m_new)
                shift = jnp.where(chosen, m_new, POS_INF)
            p = jnp.exp2(s - shift).astype(BF16)
            pv = jnp.dot(values_aug(block, h), p, preferred_element_type=F32)
            acc_ref[x, h, u] = acc_ref[x, h, u] * jnp.exp2(m_old - m_new) + pv

        for x in range(2):
            for h in range(n_heads):
                ms = [m_run[(x, h, u)] for u in range(chains)]
                m_all = functools.reduce(jnp.maximum, ms)
                merged_ref[x, h] = functools.reduce(
                    jnp.add, [acc_ref[x, h, u] * jnp.exp2(ms[u] - m_all) for u in range(chains)])
        for key, sel in next_sel.items():
            sel_ref[(1 - slot,) + key] = sel
        return carry

    lax.fori_loop(0, n_pairs, tile_pair, 0)
    write_merged(n_pairs - 1)


def _moba_attention(qt, k, vt, kmean, *, chains, pairs, lookahead):
    B, n_pairs, nb, _, blk = qt.shape
    S = k.shape[1]
    hd = k.shape[2]
    assert nb % 2 == 0 and n_pairs % pairs == 0
    n_heads = 2 * pairs
    width = pairs * LANES
    kernel = functools.partial(_moba_kernel, nb=nb, chains=chains, pairs=pairs, lookahead=lookahead)
    return pl.pallas_call(
        kernel,
        grid=(B, n_pairs // pairs),
        in_specs=[
            pl.BlockSpec((1, pairs, nb, LANES, blk), lambda b, p: (b, p, 0, 0, 0)),
            pl.BlockSpec((1, S, width), lambda b, p: (b, 0, p)),
            pl.BlockSpec((1, pairs, nb, LANES, blk), lambda b, p: (b, p, 0, 0, 0)),
            pl.BlockSpec((1, nb, width), lambda b, p: (b, 0, p)),
        ],
        out_specs=pl.BlockSpec((1, S, width), lambda b, p: (b, 0, p)),
        out_shape=jax.ShapeDtypeStruct((B, S, hd), BF16),
        scratch_shapes=[
            pltpu.VMEM((2, 2, n_heads, nb, blk), F32),
            pltpu.VMEM((2, n_heads, chains, HEAD_DIM + BF16_SUBLANES, blk), F32),
            pltpu.VMEM((2, n_heads, HEAD_DIM + BF16_SUBLANES, blk), F32),
        ],
        compiler_params=_compiler_params(("parallel", "parallel")),
        name="moba_attention",
    )(qt, k, vt, kmean)


def _qkv_dil_kernel(x_ref, w_ref, cq_ref, sq_ref, ck_ref, sk_ref, *refs, hd, dilations, tm):
    out_refs, scr_ref = refs[:-1], refs[-1]
    gw = hd // len(dilations)
    xb = x_ref[...].astype(BF16)
    qkv = jnp.dot(xb, w_ref[...], preferred_element_type=F32)
    slot = 0
    for g, d in enumerate(dilations):
        for kind in range(3):
            ref = out_refs[3 * g + kind]
            for slab in range(gw // LANES):
                col = kind * hd + g * gw + slab * LANES
                t = qkv[:, col:col + LANES]
                if kind == 0:
                    t = t * cq_ref[...] + pltpu.roll(t, HEAD_DIM, axis=1) * sq_ref[...]
                elif kind == 1:
                    t = t * ck_ref[...] + pltpu.roll(t, HEAD_DIM, axis=1) * sk_ref[...]
                lanes = slice(slab * LANES, (slab + 1) * LANES)
                if d == 1:
                    ref[0, 0, :, lanes] = t.astype(BF16)
                else:
                    scr_ref[slot] = t
                    for r in range(d):
                        ref[0, r, :, lanes] = scr_ref[slot, pl.ds(r, tm // d, stride=d), :].astype(BF16)
                    slot += 1


def _qkv_dil(x2d, w, cq, sq, ck, sk, *, tm, batch, seq, dilations):
    n, d_model = x2d.shape
    hd = w.shape[1] // 3
    gw = hd // len(dilations)
    tiles_per_seq = seq // tm
    kernel = functools.partial(_qkv_dil_kernel, hd=hd, dilations=dilations, tm=tm)
    tab = pl.BlockSpec((tm, LANES), lambda t: (t % tiles_per_seq, 0))
    out_specs, out_shape = [], []
    for d in dilations:
        for _ in range(3):
            out_specs.append(pl.BlockSpec((1, d, tm // d, gw),
                                          lambda t: (t // tiles_per_seq, 0, t % tiles_per_seq, 0)))
            out_shape.append(jax.ShapeDtypeStruct((batch, d, seq // d, gw), BF16))
    n_slots = sum(3 * (gw // LANES) for d in dilations if d > 1)
    return pl.pallas_call(
        kernel,
        grid=(n // tm,),
        in_specs=[pl.BlockSpec((tm, d_model), lambda t: (t, 0)),
                  _resident((d_model, 3 * hd), lambda t: (0, 0)),
                  tab, tab, tab, tab],
        out_specs=out_specs,
        out_shape=out_shape,
        scratch_shapes=[pltpu.VMEM((max(n_slots, 1), tm, LANES), F32)],
        compiler_params=_compiler_params(("parallel",)),
        name="qkv_dilated",
    )(x2d, w, cq, sq, ck, sk)


def _dilated_kernel(q_ref, kc_ref, kh_ref, vc_ref, vh_ref, o_ref, lse_ref, *, tl, span):
    wb = DIL_BLOCK
    n_blk = pl.program_id(2)
    n_res = q_ref.shape[0]
    q = [q_ref[c] for c in range(n_res)]
    k_all = [jnp.concatenate([kh_ref[c], kc_ref[c]], axis=0) for c in range(n_res)]
    v_all = [jnp.concatenate([vh_ref[c], vc_ref[c]], axis=0) for c in range(n_res)]
    lane = lax.broadcasted_iota(jnp.int32, (wb, LANES), 1)
    first_head_lanes = (lane % HEAD_DIM) < HALF_DIM
    first_head_out = lane < HEAD_DIM
    qq = lax.broadcasted_iota(jnp.int32, (wb, 2 * wb), 0)
    kk = lax.broadcasted_iota(jnp.int32, (wb, 2 * wb), 1)
    dist = qq + wb - kk
    band = jnp.logical_and(dist >= 0, dist <= span)
    nt_dims = (((1,), (1,)), ((), ()))
    units = [(c, pair, sub, h) for c in range(n_res) for pair in range(q[0].shape[1] // LANES)
             for sub in range(tl // wb) for h in range(2)]

    def unit_scores(c, pair, sub, h):
        lo, hi = pair * LANES, (pair + 1) * LANES
        qs = q[c][sub * wb:(sub + 1) * wb, lo:hi]
        ks = k_all[c][sub * wb:sub * wb + 2 * wb, lo:hi]
        keep = first_head_lanes if h == 0 else jnp.logical_not(first_head_lanes)
        qm = jnp.where(keep, qs, jnp.zeros_like(qs))
        return lax.dot_general(qm, ks, nt_dims, preferred_element_type=F32)

    ahead = DIL_LOOKAHEAD
    scores = {n: unit_scores(*units[n]) for n in range(ahead)}
    outs, lses = {}, {}
    for n, (c, pair, sub, h) in enumerate(units):
        lo, hi = pair * LANES, (pair + 1) * LANES
        s = scores.pop(n)
        if n + ahead < len(units):
            scores[n + ahead] = unit_scores(*units[n + ahead])
        mask = jnp.logical_and(band, jnp.logical_or(kk >= wb, n_blk > 0)) if sub == 0 else band
        s = jnp.where(mask, s, NEG_INF)
        m = jnp.max(s, axis=1, keepdims=True)
        p = jnp.exp2(s - m)
        l = jnp.sum(p, axis=1, keepdims=True)
        vs = v_all[c][sub * wb:sub * wb + 2 * wb, lo:hi]
        pv = jnp.dot(p.astype(BF16), vs, preferred_element_type=F32)
        outs[h] = pv * (1.0 / l)
        lses[h] = m + jnp.log2(l)
        if h == 1:
            rows = slice(sub * wb, (sub + 1) * wb)
            o_ref[c, rows, lo:hi] = jnp.where(first_head_out, outs[0], outs[1]).astype(BF16)
            lse_ref[c, rows, lo:hi] = jnp.where(
                first_head_out, jnp.broadcast_to(lses[0], (wb, LANES)), jnp.broadcast_to(lses[1], (wb, LANES)))


def _dilated_group(q, k, v, *, group, window, dilation):
    batch, _, L, gw = q.shape
    span = window // dilation
    assert span <= DIL_BLOCK and L % DIL_BLOCK == 0
    tl = min(DIL_ROW_TILE, L)
    halo_per_tile = tl // DIL_BLOCK
    n_res = max(1, min(dilation, DIL_ROW_TILE // tl))
    cur = pl.BlockSpec((None, n_res, tl, gw), lambda b, r, n: (b, r, n, 0))
    halo = pl.BlockSpec((None, n_res, DIL_BLOCK, gw),
                        lambda b, r, n: (b, r, jnp.maximum(n * halo_per_tile - 1, 0), 0))
    kernel = functools.partial(_dilated_kernel, tl=tl, span=span)
    return pl.pallas_call(
        kernel,
        grid=(batch, dilation // n_res, L // tl),
        in_specs=[cur, cur, halo, cur, halo],
        out_specs=[cur, cur],
        out_shape=[jax.ShapeDtypeStruct(q.shape, BF16), jax.ShapeDtypeStruct(q.shape, F32)],
        compiler_params=_compiler_params(("parallel", "parallel", "arbitrary")),
        name=f"dilated_attention_g{group}",
    )(q, k, k, v, v)


def _layer_norm(y, g, b):
    mu = jnp.mean(y, axis=-1, keepdims=True)
    yc = y - mu
    var = jnp.mean(yc * yc, axis=-1, keepdims=True)
    return yc * lax.rsqrt(var + LN_EPS) * g + b


def _position_order(ref, d, scr_ref, slot, tm):
    if d == 1:
        return ref[0, 0]
    n_slabs = ref.shape[-1] // LANES
    for slab in range(n_slabs):
        for r in range(d):
            scr_ref[slot + slab, pl.ds(r, tm // d, stride=d), :] = (
                ref[0, r, :, slab * LANES:(slab + 1) * LANES].astype(F32))
    return jnp.concatenate([scr_ref[slot + slab] for slab in range(n_slabs)], axis=1)


def _tail_kernel(*refs, dilations, tm, ff_chunk):
    n_groups = len(dilations)
    n_mix = 2 * n_groups if n_groups > 1 else 1
    mix_refs = refs[:n_mix]
    (x_ref, wo_ref, g1_ref, b1_ref, win_ref, wout_ref, g2_ref, b2_ref, out_ref) = refs[n_mix:n_mix + 9]
    scr_ref = refs[n_mix + 9] if len(refs) > n_mix + 9 else None
    if n_groups == 1:
        o = _position_order(mix_refs[0], dilations[0], scr_ref, 0, tm)
    else:
        slabs = mix_refs[0].shape[-1] // LANES
        outs = [_position_order(mix_refs[g], d, scr_ref, 2 * g * slabs, tm).astype(F32)
                for g, d in enumerate(dilations)]
        lses = [_position_order(mix_refs[n_groups + g], d, scr_ref, (2 * g + 1) * slabs, tm)
                for g, d in enumerate(dilations)]
        top = functools.reduce(jnp.maximum, lses)
        ws = [jnp.exp2(l - top) for l in lses]
        inv = 1.0 / functools.reduce(jnp.add, ws)
        o = jnp.concatenate([(outs[g] * (ws[g] * inv)).astype(BF16) for g in range(n_groups)], axis=1)
    d_ff = win_ref.shape[1]
    rows = tm // TAIL_SUBTILES

    def mlp(hb):
        acc = jnp.zeros(hb.shape, F32)
        for c in range(d_ff // ff_chunk):
            lo, hi = c * ff_chunk, (c + 1) * ff_chunk
            hid = jnp.maximum(jnp.dot(hb, win_ref[:, lo:hi], preferred_element_type=F32), 0.0)
            acc = acc + jnp.dot((hid * hid).astype(BF16), wout_ref[lo:hi, :], preferred_element_type=F32)
        return acc

    sub = [slice(r * rows, (r + 1) * rows) for r in range(TAIL_SUBTILES)]
    mixes = [jnp.dot(o[rs], wo_ref[...], preferred_element_type=F32) for rs in sub]
    hs, mlps = [], []
    for r, rs in enumerate(sub):
        hs.append(_layer_norm(DEEPNORM_ALPHA * x_ref[rs, :] + mixes[r], g1_ref[...], b1_ref[...]))
        mlps.append(mlp(hs[r].astype(BF16)))
    for r, rs in enumerate(sub):
        out_ref[rs, :] = _layer_norm(DEEPNORM_ALPHA * hs[r] + mlps[r], g2_ref[...], b2_ref[...])


def _layer_tail(mix_inputs, dilations, x2d, wo, g1, b1, win, wout, g2, b2, *, tm, seq):
    n, d_model = x2d.shape
    d_ff = win.shape[1]
    tiles_per_seq = seq // tm
    n_groups = len(dilations)

    def mix_spec(a, d):
        return pl.BlockSpec((1, d, tm // d, a.shape[-1]),
                            lambda t: (t // tiles_per_seq, 0, t % tiles_per_seq, 0))

    vec = _resident((1, d_model), lambda t: (0, 0))
    kernel = functools.partial(_tail_kernel, dilations=dilations, tm=tm, ff_chunk=min(TAIL_FF_CHUNK, d_ff))
    slabs = mix_inputs[0].shape[-1] // LANES
    n_slots = 2 * n_groups * slabs if any(d > 1 for d in dilations) else 0
    return pl.pallas_call(
        kernel,
        grid=(n // tm,),
        in_specs=[mix_spec(a, dilations[i % n_groups]) for i, a in enumerate(mix_inputs)] + [
            pl.BlockSpec((tm, d_model), lambda t: (t, 0)),
            _resident(wo.shape, lambda t: (0, 0)),
            vec, vec,
            _resident(win.shape, lambda t: (0, 0)),
            _resident(wout.shape, lambda t: (0, 0)),
            vec, vec,
        ],
        out_specs=pl.BlockSpec((tm, d_model), lambda t: (t, 0)),
        out_shape=jax.ShapeDtypeStruct((n, d_model), F32),
        scratch_shapes=[pltpu.VMEM((n_slots, tm, LANES), F32)] if n_slots else [],
        compiler_params=_compiler_params(
            ("parallel",), [False] * (len(mix_inputs) + 1) + [True, False, False, True, True, False, False]),
        name=f"layer_tail_{n_groups}",
    )(*mix_inputs, x2d, wo, g1, b1, win, wout, g2, b2)


def kernel(x, moba_w_qkv, moba_w_o, dil_w_qkv, dil_w_o, mlp_w_in, mlp_w_out,
           ln_mix_g, ln_mix_b, ln_mlp_g, ln_mlp_b):
    B, S, D = x.shape
    n_tok = B * S
    tm = min(TAIL_ROW_TILE, S)
    tm_proj = min(PROJ_ROW_TILE, S)
    q_scale = HEAD_DIM ** -0.5 * LOG2E
    cq, sq = _rotary_tables(S, q_scale)
    ck, sk = _rotary_tables(S, 1.0)
    vec = lambda a: a.reshape(1, D).astype(F32)

    w = moba_w_qkv[0]
    hd = w.shape[1] // 3
    wq = _pair_heads(w[:, :hd])
    wk = _pair_heads(w[:, hd:2 * hd])
    wv = w[:, 2 * hd:]
    k, qt, vt, kmean = _qkv_moba(x, wk.astype(BF16), wq.T.astype(BF16), wv.T.astype(BF16),
                                 ck, sk, *_rotary_tables(S, q_scale, transposed=True), tm=tm_proj)
    attn = _moba_attention(qt, k, vt, kmean.reshape(B, S // MOBA_BLOCK, hd),
                           chains=MOBA_CHAINS, pairs=1, lookahead=MOBA_LOOKAHEAD)
    h = _layer_tail([attn.reshape(B, 1, S, hd)], (1,), x.reshape(n_tok, D), moba_w_o[0].astype(BF16),
                    vec(ln_mix_g[0]), vec(ln_mix_b[0]), mlp_w_in[0].astype(BF16), mlp_w_out[0].astype(BF16),
                    vec(ln_mlp_g[0]), vec(ln_mlp_b[0]), tm=tm, seq=S)

    w = dil_w_qkv[0]
    hd = w.shape[1] // 3
    w_all = jnp.concatenate([_pair_heads(w[:, :hd]), _pair_heads(w[:, hd:2 * hd]), w[:, 2 * hd:]], axis=1)
    dilations = tuple(d for _, d in DIL_GROUPS)
    qkv = _qkv_dil(h, w_all.astype(BF16), cq, sq, ck, sk, tm=tm_proj, batch=B, seq=S, dilations=dilations)
    outs, lses = [], []
    for g, (window, dilation) in enumerate(DIL_GROUPS):
        o_g, lse_g = _dilated_group(*qkv[3 * g:3 * g + 3], group=g, window=window, dilation=dilation)
        outs.append(o_g)
        lses.append(lse_g)
    h = _layer_tail(outs + lses, dilations, h, dil_w_o[0].astype(BF16),
                    vec(ln_mix_g[1]), vec(ln_mix_b[1]), mlp_w_in[1].astype(BF16), mlp_w_out[1].astype(BF16),
                    vec(ln_mlp_g[1]), vec(ln_mlp_b[1]), tm=tm, seq=S)
    return h.reshape(B, S, D)
```
